```python
import math
import jax, jax.numpy as jnp
from jax import lax
import numpy as np

D_MODEL = 1024
BATCH = 4
SEQ = 8192
DEPTH = 1

PLE_DIM = 256
HG_HEADS = 8
HG_KEY = 128
HG_VAL = 128
HG_KWIDTH = HG_HEADS * HG_KEY
HG_WIDTH = HG_HEADS * HG_VAL
HG_CHUNK = 64
S5_GROUP = 16
S5_WIDTH = 512
S5_GROUPS = S5_WIDTH // S5_GROUP
S5_STATE = 64
DT_MIN = 0.001
DT_MAX = 0.1
NORM_EPS = 1e-6
IN_SIZES = (HG_KWIDTH, HG_KWIDTH, HG_WIDTH, HG_WIDTH, S5_WIDTH, S5_WIDTH, D_MODEL, D_MODEL)
IN_COLS = HG_KWIDTH * 2 + HG_WIDTH * 2 + S5_WIDTH * 2 + D_MODEL * 2

kernel_name = "hybrid_hgrn2_s5_gated_block"


def _split_points():
    pts, acc = [], 0
    for s in IN_SIZES[:-1]:
        acc += s
        pts.append(acc)
    return pts


def rms_norm(x, g):
    xf = x.astype(jnp.float32)
    y = xf * lax.rsqrt(jnp.mean(xf * xf, axis=-1, keepdims=True) + NORM_EPS)
    return (y * g.astype(jnp.float32)).astype(x.dtype)


def hgrn2_mix(q, f_logit, iv, lb):
    f32 = jnp.float32
    bsz, seq = q.shape[0], q.shape[1]
    nc = seq // HG_CHUNK
    lb = lb.reshape(HG_HEADS, HG_KEY).astype(f32)
    sig = jax.nn.sigmoid(f_logit.astype(f32))
    log_f = jnp.log(lb + (1.0 - lb) * sig)
    k = (1.0 - lb) * (1.0 - sig)
    chunk = lambda t: t.astype(f32).reshape(bsz, nc, HG_CHUNK, HG_HEADS, t.shape[-1])
    q, k, iv, log_f = chunk(q), chunk(k), chunk(iv), chunk(log_f)
    b = jnp.cumsum(log_f, axis=2)
    b_mid = b[:, :, HG_CHUNK // 2 - 1:HG_CHUNK // 2]
    b_last = b[:, :, HG_CHUNK - 1:HG_CHUNK]
    scores = jnp.einsum('bnthk,bnshk->bnhts', q * jnp.exp(b - b_mid), k * jnp.exp(b_mid - b))
    causal = jnp.tril(jnp.ones((HG_CHUNK, HG_CHUNK), dtype=bool))
    scores = jnp.where(causal, scores, 0.0)
    o_intra = jnp.einsum('bnhts,bnshv->bnthv', scores, iv)
    q_dec = q * jnp.exp(b)
    k_dec = k * jnp.exp(b_last - b)
    chunk_decay = jnp.exp(b_last[:, :, 0])

    def step(state, xs):
        qc, kc, ic, dc = xs
        o = jnp.einsum('bthk,bhkv->bthv', qc, state)
        state = dc[..., None] * state + jnp.einsum('bthk,bthv->bhkv', kc, ic)
        return state, o

    s0 = jnp.zeros((bsz, HG_HEADS, HG_KEY, HG_VAL), f32)
    mv = lambda t: jnp.moveaxis(t, 1, 0)
    _, o_inter = lax.scan(step, s0, (mv(q_dec), mv(k_dec), mv(iv), mv(chunk_decay)))
    o = o_intra + jnp.moveaxis(o_inter, 0, 1)
    return o.reshape(bsz, seq, HG_HEADS, HG_VAL)


def _cmul_combine(e1, e2):
    a1r, a1i, x1r, x1i = e1
    a2r, a2i, x2r, x2i = e2
    return (a1r * a2r - a1i * a2i,
            a1r * a2i + a1i * a2r,
            a2r * x1r - a2i * x1i + x2r,
            a2r * x1i + a2i * x1r + x2i)


def s5_mix(u, a_re, a_im, log_dt, b_re, b_im, c_re, c_im, d):
    f32 = jnp.float32
    uf = u.astype(f32)
    seq = u.shape[1]
    dt = jnp.exp(log_dt.astype(f32))[:, None]
    ar, ai = a_re.astype(f32), a_im.astype(f32)
    mag = jnp.exp(ar * dt)
    lr, li = mag * jnp.cos(ai * dt), mag * jnp.sin(ai * dt)
    den = ar * ar + ai * ai
    nr = lr - 1.0
    sr = (nr * ar + li * ai) / den
    si = (li * ar - nr * ai) / den
    br, bi = b_re.astype(f32), b_im.astype(f32)
    bbr = sr[..., None] * br - si[..., None] * bi
    bbi = sr[..., None] * bi + si[..., None] * br
    xr = jnp.einsum('bsgc,gnc->bsgn', uf, bbr)
    xi = jnp.einsum('bsgc,gnc->bsgn', uf, bbi)
    shape = (1, seq) + lr.shape
    lam_r = jnp.broadcast_to(lr[None, None], shape)
    lam_i = jnp.broadcast_to(li[None, None], shape)
    _, _, hr, hi = lax.associative_scan(_cmul_combine, (lam_r, lam_i, xr, xi), axis=1)
    y = (jnp.einsum('bsgn,gcn->bsgc', hr, c_re.astype(f32))
         - jnp.einsum('bsgn,gcn->bsgc', hi, c_im.astype(f32))
         + d.astype(f32) * uf)
    return y


def setup_inputs(seed: int = 0) -> dict:
    key = jax.random.key(seed)
    ks = jax.random.split(key, 24)
    n = lambda k, shape, scale: jax.random.normal(k, shape, jnp.float32) * scale
    L, G, N, C = DEPTH, S5_GROUPS, S5_STATE, S5_GROUP
    a_im_base = jnp.pi * jnp.arange(N, dtype=jnp.float32)
    return {
        "x": n(ks[0], (BATCH, SEQ, D_MODEL), 1.0),
        "p": n(ks[1], (DEPTH, BATCH, SEQ, PLE_DIM), 1.0),
        "norm_g": 1.0 + n(ks[2], (L, D_MODEL), 0.02),
        "w_in": n(ks[3], (L, D_MODEL, IN_COLS), D_MODEL ** -0.5),
        "hg_lb": n(ks[4], (DEPTH + 1, HG_KWIDTH), 0.1),
        "hg_norm_g": 1.0 + n(ks[5], (L, HG_WIDTH), 0.02),
        "w_o_hg": n(ks[6], (L, HG_WIDTH, D_MODEL), HG_WIDTH ** -0.5),
        "s5_a_re": -0.5 + n(ks[7], (L, G, N), 0.01),
        "s5_a_im": a_im_base[None, None, :] + n(ks[8], (L, G, N), 0.01),
        "s5_log_dt": jax.random.uniform(ks[9], (L, G), jnp.float32, math.log(DT_MIN), math.log(DT_MAX)),
        "s5_b_re": n(ks[10], (L, G, N, C), (2.0 * C) ** -0.5),
        "s5_b_im": n(ks[11], (L, G, N, C), (2.0 * C) ** -0.5),
        "s5_c_re": n(ks[12], (L, G, C, N), N ** -0.5),
        "s5_c_im": n(ks[13], (L, G, C, N), N ** -0.5),
        "s5_d": n(ks[14], (L, G, C), 1.0),
        "w_glu": n(ks[15], (L, S5_WIDTH, 2 * S5_WIDTH), S5_WIDTH ** -0.5),
        "b_glu": n(ks[16], (L, 2 * S5_WIDTH), 0.01),
        "w_o_s5": n(ks[17], (L, S5_WIDTH, D_MODEL), S5_WIDTH ** -0.5),
        "w_out": n(ks[18], (L, D_MODEL, D_MODEL), D_MODEL ** -0.5),
        "ple_norm_g": 1.0 + n(ks[19], (L, D_MODEL), 0.02),
        "w_ple": n(ks[20], (L, PLE_DIM, D_MODEL), PLE_DIM ** -0.5),
        "w_ple_gate": n(ks[21], (L, D_MODEL, D_MODEL), D_MODEL ** -0.5),
        "final_norm_g": 1.0 + n(ks[22], (D_MODEL,), 0.02),
    }


def reference(x, p, norm_g, w_in, hg_lb, hg_norm_g, w_o_hg, s5_a_re, s5_a_im, s5_log_dt,
              s5_b_re, s5_b_im, s5_c_re, s5_c_im, s5_d, w_glu, b_glu, w_o_s5, w_out,
              ple_norm_g, w_ple, w_ple_gate, final_norm_g):
    bsz, seq = x.shape[0], x.shape[1]
    h = x
    lb_all = jnp.cumsum(jax.nn.softmax(hg_lb.astype(jnp.float32), axis=0), axis=0)
    for l in range(DEPTH):
        u = rms_norm(h, norm_g[l])
        proj = u @ w_in[l]
        q, f_logit, iv, g_hg, u_s, z_s, gate_hg, gate_s5 = jnp.split(proj, _split_points(), axis=-1)

        o = hgrn2_mix(q.reshape(bsz, seq, HG_HEADS, HG_KEY),
                      f_logit.reshape(bsz, seq, HG_HEADS, HG_KEY),
                      iv.reshape(bsz, seq, HG_HEADS, HG_VAL), lb_all[l])
        o = rms_norm(o, hg_norm_g[l].reshape(HG_HEADS, HG_VAL)).reshape(bsz, seq, HG_WIDTH)
        y_hg = ((o * jax.nn.silu(g_hg.astype(jnp.float32))).astype(h.dtype) @ w_o_hg[l]).astype(h.dtype)

        ys = s5_mix(u_s.reshape(bsz, seq, S5_GROUPS, S5_GROUP), s5_a_re[l], s5_a_im[l], s5_log_dt[l],
                    s5_b_re[l], s5_b_im[l], s5_c_re[l], s5_c_im[l], s5_d[l]).reshape(bsz, seq, S5_WIDTH)
        ys = jax.nn.gelu(ys).astype(h.dtype)
        glu_a, glu_b = jnp.split(ys @ w_glu[l] + b_glu[l], 2, axis=-1)
        ys = glu_a * jax.nn.sigmoid(glu_b) * jax.nn.silu(z_s)
        y_s5 = (ys @ w_o_s5[l]).astype(h.dtype)

        merged = jax.nn.sigmoid(gate_hg) * y_hg + jax.nn.sigmoid(gate_s5) * y_s5
        h = h + (merged @ w_out[l]).astype(h.dtype)

        pe = p[l] @ w_ple[l]
        gate = jax.nn.sigmoid(rms_norm(h, ple_norm_g[l]) @ w_ple_gate[l])
        h = h + (pe * gate).astype(h.dtype)
    return rms_norm(h, final_norm_g)
```

```python
import functools

import jax
import jax.numpy as jnp
from jax import lax
from jax.experimental import pallas as pl
from jax.experimental.pallas import tpu as pltpu

D_MODEL = 1024
PLE_DIM = 256
HG_HEADS = 8
HG_KEY = 128
HG_VAL = 128
HG_WIDTH = HG_HEADS * HG_VAL
HG_CHUNK = 64
S5_GROUP = 16
S5_WIDTH = 512
S5_GROUPS = S5_WIDTH // S5_GROUP
S5_STATE = 64
S5_STATES = S5_GROUPS * S5_STATE
NORM_EPS = 1e-6

COL_HG = 0
COL_S5 = 4 * HG_WIDTH
COL_GATE = COL_S5 + 2 * S5_WIDTH
IN_COLS = COL_GATE + 2 * D_MODEL

SUBLANES = 8
TS = 256
SCAN_W = 256
VMEM_LIMIT_BYTES = 58 * 1024 * 1024

F32 = jnp.float32
BF16 = jnp.bfloat16


def _dot(a, b):
    return jnp.dot(a, b, preferred_element_type=F32)


def _dot_nt(a, b):
    return lax.dot_general(a, b, (((1,), (1,)), ((), ())), preferred_element_type=F32)


def _dot_tn(a, b):
    return lax.dot_general(a, b, (((0,), (0,)), ((), ())), preferred_element_type=F32)


def _rms(x):
    return x * lax.rsqrt(jnp.mean(x * x, axis=-1, keepdims=True) + NORM_EPS)


def _block_kernel(x_ref, p_ref, w_in_ref, w_o_hg_ref, w_glu_ref, w_o_s5_ref, w_out_ref, w_ple_ref,
                  w_ple_gate_ref, w_b_ref, w_c_ref, tri_ref, lam_ref, norm_g_ref, lb_ref, hg_norm_g_ref,
                  b_glu_ref, s5_d_ref, ple_norm_g_ref, final_norm_g_ref,
                  out_ref,
                  u_bf, proj, kscr, o_scr, s5p, xs, state, carry):
    ts = x_ref.shape[0]

    @pl.when(pl.program_id(1) == 0)
    def _():
        state[...] = jnp.zeros_like(state)
        carry[...] = jnp.zeros_like(carry)

    u_bf[...] = (_rms(x_ref[...]) * norm_g_ref[...]).astype(BF16)

    for j in range(4):
        cols = slice(COL_HG + j * HG_WIDTH, COL_HG + (j + 1) * HG_WIDTH)
        proj[:, j * HG_WIDTH:(j + 1) * HG_WIDTH] = _dot(u_bf[...], w_in_ref[:, cols])

    lb = lb_ref[...]
    sig = jax.nn.sigmoid(proj[:, HG_WIDTH:2 * HG_WIDTH])
    kscr[...] = (1.0 - lb) * (1.0 - sig)
    logf = jnp.log(lb + (1.0 - lb) * sig)
    logf_hi = logf.astype(BF16)
    logf_lo = (logf - logf_hi.astype(F32)).astype(BF16)
    proj[:, HG_WIDTH:2 * HG_WIDTH] = _dot(tri_ref[...], logf_hi) + _dot(tri_ref[...], logf_lo)

    row = lax.broadcasted_iota(jnp.int32, (HG_CHUNK, HG_CHUNK), 0)
    col = lax.broadcasted_iota(jnp.int32, (HG_CHUNK, HG_CHUNK), 1)
    causal = row >= col

    def chunk_body(c, _):
        r0 = pl.multiple_of(c * HG_CHUNK, HG_CHUNK)
        rows = pl.ds(r0, HG_CHUNK)
        q = proj[rows, 0:HG_WIDTH]
        b = proj[rows, HG_WIDTH:2 * HG_WIDTH]
        iv = proj[rows, 2 * HG_WIDTH:3 * HG_WIDTH].astype(BF16)
        b_mid = b[HG_CHUNK // 2 - 1:HG_CHUNK // 2, :]
        b_last = b[HG_CHUNK - 1:HG_CHUNK, :]
        qa = q * jnp.exp(b - b_mid)
        ka = kscr[rows, :] * jnp.exp(b_mid - b)
        q_dec = (qa * jnp.exp(b_mid)).astype(BF16)
        k_dec = (ka * jnp.exp(b_last - b_mid)).astype(BF16)
        decay = jnp.exp(b_last)
        qa = qa.astype(BF16)
        ka = ka.astype(BF16)
        for h in range(HG_HEADS):
            sl = slice(h * HG_KEY, (h + 1) * HG_KEY)
            scores = jnp.where(causal, _dot_nt(qa[:, sl], ka[:, sl]), 0.0)
            st = state[h]
            o = _dot(scores.astype(BF16), iv[:, sl]) + _dot_nt(q_dec[:, sl], st.astype(BF16))
            state[h] = st * decay[:, sl] + _dot_tn(iv[:, sl], k_dec[:, sl])
            o_scr[rows, sl] = _rms(o)
        return 0

    lax.fori_loop(0, ts // HG_CHUNK, chunk_body, 0)

    g_hg = proj[:, 3 * HG_WIDTH:4 * HG_WIDTH]
    o_act = o_scr[...] * hg_norm_g_ref[...] * (g_hg * jax.nn.sigmoid(g_hg))
    y_hg = _dot(o_act.astype(BF16), w_o_hg_ref[...])
    proj[:, 0:D_MODEL] = y_hg

    s5p[...] = _dot(u_bf[...], w_in_ref[:, COL_S5:COL_S5 + 2 * S5_WIDTH])
    us_bf = s5p[:, 0:S5_WIDTH].astype(BF16)
    half_in = S5_WIDTH // 2
    half_st = 2 * (S5_STATES // 2)
    for hf in range(2):
        xs[:, hf * half_st:(hf + 1) * half_st] = _dot(us_bf[:, hf * half_in:(hf + 1) * half_in], w_b_ref[hf])

    n_re = S5_STATES // 2

    def scan_body(j, _):
        r0 = pl.multiple_of(j * SUBLANES, SUBLANES)
        rows = pl.ds(r0, SUBLANES)
        for hf in range(2):
            for lc in range(0, n_re, SCAN_W):
                cre = slice(hf * half_st + lc, hf * half_st + lc + SCAN_W)
                cim = slice(hf * half_st + n_re + lc, hf * half_st + n_re + lc + SCAN_W)
                hr = xs[rows, cre]
                hi = xs[rows, cim]
                for k in range(3):
                    ar = lam_ref[k, :, cre]
                    ai = lam_ref[k, :, cim]
                    sr = pltpu.roll(hr, 1 << k, 0)
                    si = pltpu.roll(hi, 1 << k, 0)
                    hr, hi = hr + ar * sr - ai * si, hi + ar * si + ai * sr
                pr = lam_ref[3, :, cre]
                pi = lam_ref[3, :, cim]
                cr = carry[:, cre]
                ci = carry[:, cim]
                hr, hi = hr + pr * cr - pi * ci, hi + pr * ci + pi * cr
                xs[rows, cre] = hr
                xs[rows, cim] = hi
                carry[:, cre] = jnp.broadcast_to(hr[SUBLANES - 1:SUBLANES, :], (SUBLANES, SCAN_W))
                carry[:, cim] = jnp.broadcast_to(hi[SUBLANES - 1:SUBLANES, :], (SUBLANES, SCAN_W))
        return 0

    lax.fori_loop(0, ts // SUBLANES, scan_body, 0)

    y_parts = [_dot(xs[:, hf * half_st:(hf + 1) * half_st].astype(BF16), w_c_ref[hf]) for hf in range(2)]
    y_s = jnp.concatenate(y_parts, axis=1) + s5_d_ref[...] * s5p[:, 0:S5_WIDTH]
    glu = _dot(jax.nn.gelu(y_s).astype(BF16), w_glu_ref[...]) + b_glu_ref[...]
    z_s = s5p[:, S5_WIDTH:2 * S5_WIDTH]
    ys = glu[:, 0:S5_WIDTH] * jax.nn.sigmoid(glu[:, S5_WIDTH:2 * S5_WIDTH]) * (z_s * jax.nn.sigmoid(z_s))
    y_s5 = _dot(ys.astype(BF16), w_o_s5_ref[...])

    gate_hg = _dot(u_bf[...], w_in_ref[:, COL_GATE:COL_GATE + D_MODEL])
    gate_s5 = _dot(u_bf[...], w_in_ref[:, COL_GATE + D_MODEL:COL_GATE + 2 * D_MODEL])
    merged = jax.nn.sigmoid(gate_hg) * proj[:, 0:D_MODEL] + jax.nn.sigmoid(gate_s5) * y_s5
    h = x_ref[...] + _dot(merged.astype(BF16), w_out_ref[...])

    pe = _dot(p_ref[...].astype(BF16), w_ple_ref[...])
    gate = jax.nn.sigmoid(_dot((_rms(h) * ple_norm_g_ref[...]).astype(BF16), w_ple_gate_ref[...]))
    h = h + pe * gate
    out_ref[...] = _rms(h) * final_norm_g_ref[...]


def _s5_tables(a_re, a_im, log_dt, b_re, b_im, c_re, c_im):
    g, n, c = S5_GROUPS, S5_STATE, S5_GROUP
    dt = jnp.exp(log_dt.astype(F32))[:, None]
    ar, ai = a_re.astype(F32), a_im.astype(F32)
    mag = jnp.exp(ar * dt)
    lr, li = mag * jnp.cos(ai * dt), mag * jnp.sin(ai * dt)
    den = ar * ar + ai * ai
    nr = lr - 1.0
    sr = (nr * ar + li * ai) / den
    si = (li * ar - nr * ai) / den
    br, bi = b_re.astype(F32), b_im.astype(F32)
    bbr = sr[..., None] * br - si[..., None] * bi
    bbi = sr[..., None] * bi + si[..., None] * br
    eye = jnp.eye(g, dtype=F32)
    hs, hi_ = S5_STATES // 2, S5_WIDTH // 2

    def in_mat(bb):
        return jnp.einsum('gnc,gh->gchn', bb, eye).reshape(g * c, g * n)

    def out_mat(cc):
        return jnp.einsum('gcn,gh->gnhc', cc.astype(F32), eye).reshape(g * n, g * c)

    mbr, mbi = in_mat(bbr), in_mat(bbi)
    w_b = jnp.stack([jnp.concatenate([mbr[k * hi_:(k + 1) * hi_, k * hs:(k + 1) * hs],
                                      mbi[k * hi_:(k + 1) * hi_, k * hs:(k + 1) * hs]], axis=1)
                     for k in range(2)]).astype(BF16)
    mcr, mci = out_mat(c_re), out_mat(c_im)
    w_c = jnp.stack([jnp.concatenate([mcr[k * hs:(k + 1) * hs, k * hi_:(k + 1) * hi_],
                                      -mci[k * hs:(k + 1) * hs, k * hi_:(k + 1) * hi_]], axis=0)
                     for k in range(2)]).astype(BF16)

    def layout(re, im):
        return jnp.concatenate([re[..., :hs], im[..., :hs], re[..., hs:], im[..., hs:]], axis=-1)

    lr, li = lr.reshape(-1), li.reshape(-1)
    pows = [(lr, li)]
    for _ in range(SUBLANES - 1):
        pr, pi = pows[-1]
        pows.append((pr * lr - pi * li, pr * li + pi * lr))
    pw = jnp.stack([layout(pr, pi) for pr, pi in pows])
    rows = jnp.arange(SUBLANES)[:, None]
    steps = [jnp.where(rows >= k, pw[k - 1][None, :], 0.0) for k in (1, 2, 4)]
    lam = jnp.stack(steps + [pw]).astype(F32)
    return w_b, w_c, lam


def kernel(x, p, norm_g, w_in, hg_lb, hg_norm_g, w_o_hg, s5_a_re, s5_a_im, s5_log_dt, s5_b_re, s5_b_im,
           s5_c_re, s5_c_im, s5_d, w_glu, b_glu, w_o_s5, w_out, ple_norm_g, w_ple, w_ple_gate, final_norm_g):
    bsz, seq, d = x.shape
    depth = w_in.shape[0]
    assert depth == 1 and d == D_MODEL and seq % TS == 0 and TS % HG_CHUNK == 0
    l = 0
    lb = jnp.cumsum(jax.nn.softmax(hg_lb.astype(F32), axis=0), axis=0)[l].reshape(1, HG_WIDTH)
    w_b, w_c, lam = _s5_tables(s5_a_re[l], s5_a_im[l], s5_log_dt[l], s5_b_re[l], s5_b_im[l], s5_c_re[l], s5_c_im[l])
    r = jnp.arange(TS)
    tri = ((r[:, None] >= r[None, :]) & (r[:, None] // HG_CHUNK == r[None, :] // HG_CHUNK)).astype(BF16)

    weights = [w_in[l].astype(BF16), w_o_hg[l].astype(BF16), w_glu[l].astype(BF16), w_o_s5[l].astype(BF16),
               w_out[l].astype(BF16), w_ple[l].astype(BF16), w_ple_gate[l].astype(BF16), w_b, w_c, tri, lam]
    rows_ = [norm_g[l].reshape(1, d), lb, hg_norm_g[l].reshape(1, HG_WIDTH), b_glu[l].reshape(1, 2 * S5_WIDTH),
             s5_d[l].reshape(1, S5_WIDTH), ple_norm_g[l].reshape(1, d), final_norm_g.reshape(1, d)]
    rows_ = [a.astype(F32) for a in rows_]

    def resident(a):
        nd = a.ndim
        return pl.BlockSpec(a.shape, lambda b, s, _nd=nd: (0,) * _nd, pipeline_mode=pl.Buffered(1))

    in_specs = ([pl.BlockSpec((None, TS, d), lambda b, s: (b, s, 0)),
                 pl.BlockSpec((None, TS, PLE_DIM), lambda b, s: (b, s, 0))]
                + [resident(a) for a in weights] + [resident(a) for a in rows_])
    scratch = [
        pltpu.VMEM((TS, d), BF16),
        pltpu.VMEM((TS, 4 * HG_WIDTH), F32),
        pltpu.VMEM((TS, HG_WIDTH), F32),
        pltpu.VMEM((TS, HG_WIDTH), F32),
        pltpu.VMEM((TS, 2 * S5_WIDTH), F32),
        pltpu.VMEM((TS, 2 * S5_STATES), F32),
        pltpu.VMEM((HG_HEADS, HG_VAL, HG_KEY), F32),
        pltpu.VMEM((SUBLANES, 2 * S5_STATES), F32),
    ]
    return pl.pallas_call(
        _block_kernel,
        out_shape=jax.ShapeDtypeStruct((bsz, seq, d), x.dtype),
        grid=(bsz, seq // TS),
        in_specs=in_specs,
        out_specs=pl.BlockSpec((None, TS, d), lambda b, s: (b, s, 0)),
        scratch_shapes=scratch,
        compiler_params=pltpu.CompilerParams(dimension_semantics=("arbitrary", "arbitrary"),
                                             vmem_limit_bytes=VMEM_LIMIT_BYTES),
        name="hgrn2_s5_block",
    )(x, p[l], *weights, *rows_)
```

```python
import functools

import jax
import jax.numpy as jnp
from jax import lax
from jax.experimental import pallas as pl
from jax.experimental.pallas import tpu as pltpu

D_MODEL = 1024
PLE_DIM = 256
HG_HEADS = 8
HG_KEY = 128
HG_VAL = 128
HG_WIDTH = HG_HEADS * HG_VAL
HG_CHUNK = 64
S5_GROUP = 16
S5_WIDTH = 512
S5_GROUPS = S5_WIDTH // S5_GROUP
S5_STATE = 64
S5_STATES = S5_GROUPS * S5_STATE
NORM_EPS = 1e-6

COL_F = 0
COL_S5 = HG_WIDTH
COL_MAIN = COL_S5 + 2 * S5_WIDTH
P_Q, P_IV, P_G, P_GATE_HG, P_GATE_S5 = (k * HG_WIDTH for k in range(5))
MAIN_COLS = 5 * HG_WIDTH

SUBLANES = 8
TS = 256
SEG = TS // SUBLANES
SCAN_W = 512
MM_PANEL = 256
VMEM_LIMIT_BYTES = 58 * 1024 * 1024

F32 = jnp.float32
BF16 = jnp.bfloat16


def _dot(a, b):
    return jnp.dot(a, b, preferred_element_type=F32)


def _dot_nt(a, b):
    return lax.dot_general(a, b, (((1,), (1,)), ((), ())), preferred_element_type=F32)


def _dot_tn(a, b):
    return lax.dot_general(a, b, (((0,), (0,)), ((), ())), preferred_element_type=F32)


def _rms(x):
    return x * lax.rsqrt(jnp.mean(x * x, axis=-1, keepdims=True) + NORM_EPS)


def _cmul_add(ar, ai, br, bi, cr, ci):
    return ar * br - ai * bi + cr, ar * bi + ai * br + ci


def _interleave(*streams):
    totals = [sum(c for c, _ in s) or 1.0 for s in streams]
    pos = [0] * len(streams)
    done = [0.0] * len(streams)
    while True:
        live = [i for i, s in enumerate(streams) if pos[i] < len(s)]
        if not live:
            return
        i = min(live, key=lambda k: done[k] / totals[k])
        cost, thunk = streams[i][pos[i]]
        thunk()
        pos[i] += 1
        done[i] += cost


def _block_kernel(x_ref, p_ref, w_in_ref, w_o_hg_ref, w_glu_ref, w_o_s5_ref, w_out_ref, w_ple_ref,
                  w_ple_gate_ref, w_b_ref, w_c_ref, tri_ref, perm_ref, perm_t_ref, lam_ref, seg_ref, pw_ref,
                  norm_g_ref, lb_ref, hg_norm_g_ref, b_glu_ref, s5_d_ref, ple_norm_g_ref, final_norm_g_ref,
                  out_ref,
                  u_bf, fb, proj, kscr, o_scr, s5p, xs, state, carry, pe_s, qa_s, ka_s, qd_s, kd_s, iv_s):
    ts = x_ref.shape[0]
    n_chunks = ts // HG_CHUNK
    half_in = S5_WIDTH // 2
    half_st = 2 * (S5_STATES // 2)
    n_re = S5_STATES // 2

    @pl.when(pl.program_id(1) == 0)
    def _():
        state[...] = jnp.zeros_like(state)
        carry[...] = jnp.zeros_like(carry)

    def in_proj(dst, dst_col, src_col, n_cols):
        out = []
        for j in range(0, n_cols, MM_PANEL):
            def thunk(j=j):
                dst[:, dst_col + j:dst_col + j + MM_PANEL] = _dot(
                    u_bf[...], w_in_ref[:, src_col + j:src_col + j + MM_PANEL])
            out.append((MM_PANEL * D_MODEL, thunk))
        return out

    u_bf[...] = (_rms(x_ref[...]) * norm_g_ref[...]).astype(BF16)
    for _, thunk in in_proj(fb, 0, COL_F, HG_WIDTH):
        thunk()

    def gate_math(c):
        rows = slice(c * HG_CHUNK, (c + 1) * HG_CHUNK)
        lb = lb_ref[...]
        sig = jax.nn.sigmoid(fb[rows, :])
        kscr[rows, :] = (1.0 - lb) * (1.0 - sig)
        logf = jnp.log(lb + (1.0 - lb) * sig)
        logf_hi = logf.astype(BF16)
        qa_s[rows, :] = logf_hi
        ka_s[rows, :] = (logf - logf_hi.astype(F32)).astype(BF16)

    def s5_perm():
        qd_s[:, 0:S5_WIDTH] = _dot(perm_ref[...], s5p[:, 0:S5_WIDTH].astype(BF16)).astype(BF16)

    def s5_in(hf, j):
        cols = slice(hf * half_st + j, hf * half_st + j + 2 * MM_PANEL)
        xs[:, cols] = _dot(qd_s[:, hf * half_in:(hf + 1) * half_in], w_b_ref[hf, :, j:j + 2 * MM_PANEL])

    s5_mm = in_proj(s5p, 0, COL_S5, 2 * S5_WIDTH) + [(ts * S5_WIDTH, s5_perm)]
    s5_mm += [(2 * MM_PANEL * half_in, functools.partial(s5_in, hf, j))
              for hf in range(2) for j in range(0, half_st, 2 * MM_PANEL)]
    _interleave([(1.0, functools.partial(gate_math, c)) for c in range(n_chunks)], s5_mm)

    def cumsum(j):
        cols = slice(j, j + MM_PANEL)
        fb[:, cols] = _dot(tri_ref[...], qa_s[:, cols]) + _dot(tri_ref[...], ka_s[:, cols])

    main_mm = ([(2 * MM_PANEL * ts, functools.partial(cumsum, j)) for j in range(0, HG_WIDTH, MM_PANEL)]
               + in_proj(proj, 0, COL_MAIN, MAIN_COLS))

    lanes = [(slice(hf * half_st + lc, hf * half_st + lc + SCAN_W),
              slice(hf * half_st + n_re + lc, hf * half_st + n_re + lc + SCAN_W))
             for hf in range(2) for lc in range(0, n_re, SCAN_W)]
    h_run = [None] * len(lanes)

    def scan_local(i):
        rows = slice(i * SUBLANES, (i + 1) * SUBLANES)
        for n, (cre, cim) in enumerate(lanes):
            xr, xi = xs[rows, cre], xs[rows, cim]
            if i > 0:
                hr, hi = h_run[n]
                xr, xi = _cmul_add(lam_ref[:, cre], lam_ref[:, cim], hr, hi, xr, xi)
                xs[rows, cre] = xr
                xs[rows, cim] = xi
            h_run[n] = (xr, xi)

    n_first = (2 * len(main_mm)) // 3
    _interleave([(1.0, functools.partial(scan_local, i)) for i in range(SEG)], main_mm[:n_first])

    seg_start = [None] * len(lanes)
    sub = lax.broadcasted_iota(jnp.int32, (SUBLANES, SCAN_W), 0)
    for n, (cre, cim) in enumerate(lanes):
        er, ei = h_run[n]
        gr, gi = er, ei
        for k in range(3):
            gr, gi = _cmul_add(seg_ref[k, :, cre], seg_ref[k, :, cim],
                               pltpu.roll(gr, 1 << k, 0), pltpu.roll(gi, 1 << k, 0), gr, gi)
        gr = jnp.where(sub == 0, 0.0, pltpu.roll(gr, 1, 0))
        gi = jnp.where(sub == 0, 0.0, pltpu.roll(gi, 1, 0))
        gr, gi = _cmul_add(seg_ref[3, :, cre], seg_ref[3, :, cim], carry[:, cre], carry[:, cim], gr, gi)
        seg_start[n] = (gr, gi)
        nr, ni = _cmul_add(seg_ref[4, :, cre], seg_ref[4, :, cim], gr, gi, er, ei)
        carry[:, cre] = jnp.broadcast_to(nr[SUBLANES - 1:SUBLANES, :], (SUBLANES, SCAN_W))
        carry[:, cim] = jnp.broadcast_to(ni[SUBLANES - 1:SUBLANES, :], (SUBLANES, SCAN_W))

    def scan_fix(i):
        rows = slice(i * SUBLANES, (i + 1) * SUBLANES)
        for n, (cre, cim) in enumerate(lanes):
            gr, gi = seg_start[n]
            hr, hi = _cmul_add(pw_ref[i:i + 1, cre], pw_ref[i:i + 1, cim], gr, gi, xs[rows, cre], xs[rows, cim])
            xs[rows, cre] = hr
            xs[rows, cim] = hi

    _interleave([(1.0, functools.partial(scan_fix, i)) for i in range(SEG)], main_mm[n_first:])

    decay = [None] * n_chunks

    def chunk_math(c):
        rows = slice(c * HG_CHUNK, (c + 1) * HG_CHUNK)
        b = fb[rows, :]
        b_mid = b[HG_CHUNK // 2 - 1:HG_CHUNK // 2, :]
        b_last = b[HG_CHUNK - 1:HG_CHUNK, :]
        qa = proj[rows, P_Q:P_Q + HG_WIDTH] * jnp.exp(b - b_mid)
        ka = kscr[rows, :] * jnp.exp(b_mid - b)
        qd_s[rows, :] = (qa * jnp.exp(b_mid)).astype(BF16)
        kd_s[rows, :] = (ka * jnp.exp(b_last - b_mid)).astype(BF16)
        qa_s[rows, :] = qa.astype(BF16)
        ka_s[rows, :] = ka.astype(BF16)
        iv_s[rows, :] = proj[rows, P_IV:P_IV + HG_WIDTH].astype(BF16)
        decay[c] = jnp.exp(b_last)

    def s5_out_a():
        y_parts = [_dot(xs[:, hf * half_st:(hf + 1) * half_st].astype(BF16), w_c_ref[hf]) for hf in range(2)]
        y_p = jnp.concatenate(y_parts, axis=1)
        y_hi = y_p.astype(BF16)
        y_lo = (y_p - y_hi.astype(F32)).astype(BF16)
        y_s = _dot(perm_t_ref[...], y_hi) + _dot(perm_t_ref[...], y_lo) + s5_d_ref[...] * s5p[:, 0:S5_WIDTH]
        xs[:, 0:S5_WIDTH] = jax.nn.gelu(y_s)

    def s5_out_b():
        glu = _dot(xs[:, 0:S5_WIDTH].astype(BF16), w_glu_ref[...]) + b_glu_ref[...]
        z_s = s5p[:, S5_WIDTH:2 * S5_WIDTH]
        xs[:, 0:S5_WIDTH] = (glu[:, 0:S5_WIDTH] * jax.nn.sigmoid(glu[:, S5_WIDTH:2 * S5_WIDTH])
                             * (z_s * jax.nn.sigmoid(z_s)))

    def s5_out_c():
        xs[:, D_MODEL:2 * D_MODEL] = _dot(xs[:, 0:S5_WIDTH].astype(BF16), w_o_s5_ref[...])

    def ple():
        pe_s[...] = _dot(p_ref[...].astype(BF16), w_ple_ref[...])

    _interleave([(1.0, functools.partial(chunk_math, c)) for c in range(n_chunks)],
                [(2.0, s5_out_a), (1.0, s5_out_b), (1.0, s5_out_c), (0.5, ple)])

    row = lax.broadcasted_iota(jnp.int32, (ts, ts), 0)
    col = lax.broadcasted_iota(jnp.int32, (ts, ts), 1)
    causal = (col <= row) & (col >= (row & -HG_CHUNK))

    for h in range(HG_HEADS):
        sl = slice(h * HG_KEY, (h + 1) * HG_KEY)
        scores = jnp.where(causal, _dot_nt(qa_s[:, sl], ka_s[:, sl]), 0.0)
        o_scr[:, sl] = _dot(scores.astype(BF16), iv_s[:, sl])
    for c in range(n_chunks):
        rows = slice(c * HG_CHUNK, (c + 1) * HG_CHUNK)
        for h in range(HG_HEADS):
            sl = slice(h * HG_KEY, (h + 1) * HG_KEY)
            st = state[h]
            o = o_scr[rows, sl] + _dot_nt(qd_s[rows, sl], st.astype(BF16))
            state[h] = st * decay[c][:, sl] + _dot_tn(iv_s[rows, sl], kd_s[rows, sl])
            o_scr[rows, sl] = _rms(o)

    g_hg = proj[:, P_G:P_G + HG_WIDTH]
    o_act = o_scr[...] * hg_norm_g_ref[...] * (g_hg * jax.nn.sigmoid(g_hg))
    y_hg = _dot(o_act.astype(BF16), w_o_hg_ref[...])

    merged = (jax.nn.sigmoid(proj[:, P_GATE_HG:P_GATE_HG + D_MODEL]) * y_hg
              + jax.nn.sigmoid(proj[:, P_GATE_S5:P_GATE_S5 + D_MODEL]) * xs[:, D_MODEL:2 * D_MODEL])
    h = x_ref[...] + _dot(merged.astype(BF16), w_out_ref[...])

    gate = jax.nn.sigmoid(_dot((_rms(h) * ple_norm_g_ref[...]).astype(BF16), w_ple_gate_ref[...]))
    h = h + pe_s[...] * gate
    out_ref[...] = _rms(h) * final_norm_g_ref[...]


def _s5_tables(a_re, a_im, log_dt, b_re, b_im, c_re, c_im):
    g, n, c = S5_GROUPS, S5_STATE, S5_GROUP
    dt = jnp.exp(log_dt.astype(F32))[:, None]
    ar, ai = a_re.astype(F32), a_im.astype(F32)
    mag = jnp.exp(ar * dt)
    lr, li = mag * jnp.cos(ai * dt), mag * jnp.sin(ai * dt)
    den = ar * ar + ai * ai
    nr = lr - 1.0
    sr = (nr * ar + li * ai) / den
    si = (li * ar - nr * ai) / den
    br, bi = b_re.astype(F32), b_im.astype(F32)
    bbr = sr[..., None] * br - si[..., None] * bi
    bbi = sr[..., None] * bi + si[..., None] * br
    eye = jnp.eye(g, dtype=F32)
    hs, hi_ = S5_STATES // 2, S5_WIDTH // 2

    def in_mat(bb):
        return jnp.einsum('gnc,gh->gchn', bb, eye).reshape(g * c, g * n)

    def out_mat(cc):
        return jnp.einsum('gcn,gh->gnhc', cc.astype(F32), eye).reshape(g * n, g * c)

    mbr, mbi = in_mat(bbr), in_mat(bbi)
    w_b = jnp.stack([jnp.concatenate([mbr[k * hi_:(k + 1) * hi_, k * hs:(k + 1) * hs],
                                      mbi[k * hi_:(k + 1) * hi_, k * hs:(k + 1) * hs]], axis=1)
                     for k in range(2)]).astype(BF16)
    mcr, mci = out_mat(c_re), out_mat(c_im)
    w_c = jnp.stack([jnp.concatenate([mcr[k * hs:(k + 1) * hs, k * hi_:(k + 1) * hi_],
                                      -mci[k * hs:(k + 1) * hs, k * hi_:(k + 1) * hi_]], axis=0)
                     for k in range(2)]).astype(BF16)

    def layout(z):
        re, im = z
        return jnp.concatenate([re[:hs], im[:hs], re[hs:], im[hs:]])

    def cmul(a, b):
        return a[0] * b[0] - a[1] * b[1], a[0] * b[1] + a[1] * b[0]

    lam1 = (lr.reshape(-1), li.reshape(-1))
    pows = [lam1]
    for _ in range(SEG - 1):
        pows.append(cmul(pows[-1], lam1))
    lam_seg = pows[-1]
    seg_pows = [(jnp.ones_like(lam1[0]), jnp.zeros_like(lam1[0]))]
    for _ in range(SUBLANES - 1):
        seg_pows.append(cmul(seg_pows[-1], lam_seg))
    rows = jnp.arange(SUBLANES)[:, None]
    bcast = lambda z: jnp.broadcast_to(layout(z)[None, :], (SUBLANES, 2 * S5_STATES))
    lam = bcast(lam1).astype(F32)
    seg = jnp.stack([jnp.where(rows >= k, bcast(seg_pows[k]), 0.0) for k in (1, 2, 4)]
                    + [jnp.stack([layout(z) for z in seg_pows]), bcast(lam_seg)]).astype(F32)
    pw = jnp.stack([layout(z) for z in pows]).astype(F32)
    return w_b, w_c, lam, seg, pw


def kernel(x, p, norm_g, w_in, hg_lb, hg_norm_g, w_o_hg, s5_a_re, s5_a_im, s5_log_dt, s5_b_re, s5_b_im,
           s5_c_re, s5_c_im, s5_d, w_glu, b_glu, w_o_s5, w_out, ple_norm_g, w_ple, w_ple_gate, final_norm_g):
    bsz, seq, d = x.shape
    depth = w_in.shape[0]
    assert depth == 1 and d == D_MODEL and seq % TS == 0 and TS % HG_CHUNK == 0
    l = 0
    lb = jnp.cumsum(jax.nn.softmax(hg_lb.astype(F32), axis=0), axis=0)[l].reshape(1, HG_WIDTH)
    w_b, w_c, lam, seg, pw = _s5_tables(s5_a_re[l], s5_a_im[l], s5_log_dt[l], s5_b_re[l], s5_b_im[l],
                                        s5_c_re[l], s5_c_im[l])
    r = jnp.arange(TS)
    tri = ((r[:, None] >= r[None, :]) & (r[:, None] // HG_CHUNK == r[None, :] // HG_CHUNK)).astype(BF16)
    perm = (r[None, :] == (r[:, None] % SUBLANES) * SEG + r[:, None] // SUBLANES).astype(BF16)

    w_in_l = w_in[l].astype(BF16)
    q_c, f_c, i_c, g_c, s5_c, gate_c = (slice(a, b) for a, b in (
        (0, 1024), (1024, 2048), (2048, 3072), (3072, 4096), (4096, 5120), (5120, 7168)))
    w_in_k = jnp.concatenate([w_in_l[:, c] for c in (f_c, s5_c, q_c, i_c, g_c, gate_c)], axis=1)
    weights = [w_in_k, w_o_hg[l].astype(BF16), w_glu[l].astype(BF16), w_o_s5[l].astype(BF16),
               w_out[l].astype(BF16), w_ple[l].astype(BF16), w_ple_gate[l].astype(BF16), w_b, w_c, tri,
               perm, perm.T, lam, seg, pw]
    rows_ = [norm_g[l].reshape(1, d), lb, hg_norm_g[l].reshape(1, HG_WIDTH), b_glu[l].reshape(1, 2 * S5_WIDTH),
             s5_d[l].reshape(1, S5_WIDTH), ple_norm_g[l].reshape(1, d), final_norm_g.reshape(1, d)]
    rows_ = [a.astype(F32) for a in rows_]

    def resident(a):
        nd = a.ndim
        return pl.BlockSpec(a.shape, lambda b, s, _nd=nd: (0,) * _nd, pipeline_mode=pl.Buffered(1))

    in_specs = ([pl.BlockSpec((None, TS, d), lambda b, s: (b, s, 0)),
                 pl.BlockSpec((None, TS, PLE_DIM), lambda b, s: (b, s, 0))]
                + [resident(a) for a in weights] + [resident(a) for a in rows_])
    scratch = [
        pltpu.VMEM((TS, d), BF16),
        pltpu.VMEM((TS, HG_WIDTH), F32),
        pltpu.VMEM((TS, MAIN_COLS), F32),
        pltpu.VMEM((TS, HG_WIDTH), F32),
        pltpu.VMEM((TS, HG_WIDTH), F32),
        pltpu.VMEM((TS, 2 * S5_WIDTH), F32),
        pltpu.VMEM((TS, 2 * S5_STATES), F32),
        pltpu.VMEM((HG_HEADS, HG_VAL, HG_KEY), F32),
        pltpu.VMEM((SUBLANES, 2 * S5_STATES), F32),
        pltpu.VMEM((TS, D_MODEL), F32),
    ] + [pltpu.VMEM((TS, HG_WIDTH), BF16)] * 5
    return pl.pallas_call(
        _block_kernel,
        out_shape=jax.ShapeDtypeStruct((bsz, seq, d), x.dtype),
        grid=(bsz, seq // TS),
        in_specs=in_specs,
        out_specs=pl.BlockSpec((None, TS, d), lambda b, s: (b, s, 0)),
        scratch_shapes=scratch,
        compiler_params=pltpu.CompilerParams(dimension_semantics=("arbitrary", "arbitrary"),
                                             vmem_limit_bytes=VMEM_LIMIT_BYTES),
        name="hgrn2_s5_block",
    )(x, p[l], *weights, *rows_)
```

```python
import functools

import jax
import jax.numpy as jnp
from jax import lax
from jax.experimental import pallas as pl
from jax.experimental.pallas import tpu as pltpu

D_MODEL = 1024
PLE_DIM = 256
HG_HEADS = 8
HG_KEY = 128
HG_VAL = 128
HG_WIDTH = HG_HEADS * HG_VAL
HG_CHUNK = 64
S5_GROUP = 16
S5_WIDTH = 512
S5_GROUPS = S5_WIDTH // S5_GROUP
S5_STATE = 64
S5_STATES = S5_GROUPS * S5_STATE
NORM_EPS = 1e-6

COL_F = 0
COL_S5 = HG_WIDTH
COL_MAIN = COL_S5 + 2 * S5_WIDTH
P_Q, P_IV, P_G, P_GATE_HG, P_GATE_S5 = (k * HG_WIDTH for k in range(5))
MAIN_COLS = 5 * HG_WIDTH

SUBLANES = 8
TS = 256
SEG = TS // SUBLANES
SCAN_W = 512
MM_PANEL = 256
VMEM_LIMIT_BYTES = 58 * 1024 * 1024

F32 = jnp.float32
BF16 = jnp.bfloat16


def _dot(a, b):
    return jnp.dot(a, b, preferred_element_type=F32)


def _dot_nt(a, b):
    return lax.dot_general(a, b, (((1,), (1,)), ((), ())), preferred_element_type=F32)


def _dot_tn(a, b):
    return lax.dot_general(a, b, (((0,), (0,)), ((), ())), preferred_element_type=F32)


def _rms(x):
    return x * lax.rsqrt(jnp.mean(x * x, axis=-1, keepdims=True) + NORM_EPS)


def _cmul_add(ar, ai, br, bi, cr, ci):
    return ar * br - ai * bi + cr, ar * bi + ai * br + ci


def _interleave(*streams):
    totals = [sum(c for c, _ in s) or 1.0 for s in streams]
    pos = [0] * len(streams)
    done = [0.0] * len(streams)
    while True:
        live = [i for i, s in enumerate(streams) if pos[i] < len(s)]
        if not live:
            return
        i = min(live, key=lambda k: done[k] / totals[k])
        cost, thunk = streams[i][pos[i]]
        thunk()
        pos[i] += 1
        done[i] += cost


def _block_kernel(x_ref, p_ref, w_in_ref, w_o_hg_ref, w_glu_ref, w_o_s5_ref, w_out_ref, w_ple_ref,
                  w_ple_gate_ref, w_b_ref, w_c_ref, tri_ref, perm_ref, perm_t_ref, lam_ref, seg_ref,
                  norm_g_ref, lb_ref, hg_norm_g_ref, b_glu_ref, s5_d_ref, ple_norm_g_ref, final_norm_g_ref,
                  out_ref,
                  u_bf, fb, proj, kscr, o_scr, s5p, xs, state, carry, pe_s, qa_s, ka_s, qd_s, kd_s, iv_s):
    ts = x_ref.shape[0]
    n_chunks = ts // HG_CHUNK
    half_in = S5_WIDTH // 2
    half_st = 2 * (S5_STATES // 2)
    n_re = S5_STATES // 2

    @pl.when(pl.program_id(1) == 0)
    def _():
        state[...] = jnp.zeros_like(state)
        carry[...] = jnp.zeros_like(carry)

    def in_proj(dst, dst_col, src_col, n_cols):
        out = []
        for j in range(0, n_cols, MM_PANEL):
            def thunk(j=j):
                dst[:, dst_col + j:dst_col + j + MM_PANEL] = _dot(
                    u_bf[...], w_in_ref[:, src_col + j:src_col + j + MM_PANEL])
            out.append((MM_PANEL * D_MODEL, thunk))
        return out

    u_bf[...] = (_rms(x_ref[...]) * norm_g_ref[...]).astype(BF16)
    for _, thunk in in_proj(fb, 0, COL_F, HG_WIDTH):
        thunk()

    def gate_math(c):
        rows = slice(c * HG_CHUNK, (c + 1) * HG_CHUNK)
        lb = lb_ref[...]
        sig = jax.nn.sigmoid(fb[rows, :])
        kscr[rows, :] = (1.0 - lb) * (1.0 - sig)
        logf = jnp.log(lb + (1.0 - lb) * sig)
        logf_hi = logf.astype(BF16)
        qa_s[rows, :] = logf_hi
        ka_s[rows, :] = (logf - logf_hi.astype(F32)).astype(BF16)

    def s5_perm():
        qd_s[:, 0:S5_WIDTH] = _dot(perm_ref[...], s5p[:, 0:S5_WIDTH].astype(BF16)).astype(BF16)

    def s5_in(hf, j):
        cols = slice(hf * half_st + j, hf * half_st + j + 2 * MM_PANEL)
        xs[:, cols] = _dot(qd_s[:, hf * half_in:(hf + 1) * half_in], w_b_ref[hf, :, j:j + 2 * MM_PANEL])

    s5_mm = in_proj(s5p, 0, COL_S5, 2 * S5_WIDTH) + [(ts * S5_WIDTH, s5_perm)]
    s5_mm += [(2 * MM_PANEL * half_in, functools.partial(s5_in, hf, j))
              for hf in range(2) for j in range(0, half_st, 2 * MM_PANEL)]
    _interleave([(1.0, functools.partial(gate_math, c)) for c in range(n_chunks)], s5_mm)

    def cumsum(j):
        cols = slice(j, j + MM_PANEL)
        fb[:, cols] = _dot(tri_ref[...], qa_s[:, cols]) + _dot(tri_ref[...], ka_s[:, cols])

    main_mm = ([(2 * MM_PANEL * ts, functools.partial(cumsum, j)) for j in range(0, HG_WIDTH, MM_PANEL)]
               + in_proj(proj, 0, COL_MAIN, MAIN_COLS))

    lanes = [(slice(hf * half_st + lc, hf * half_st + lc + SCAN_W),
              slice(hf * half_st + n_re + lc, hf * half_st + n_re + lc + SCAN_W))
             for hf in range(2) for lc in range(0, n_re, SCAN_W)]
    h_run = [None] * len(lanes)

    def scan_step(i, store):
        rows = slice(i * SUBLANES, (i + 1) * SUBLANES)
        for n, (cre, cim) in enumerate(lanes):
            hr, hi = xs[rows, cre], xs[rows, cim]
            if h_run[n] is not None:
                hr, hi = _cmul_add(lam_ref[:, cre], lam_ref[:, cim], *h_run[n], hr, hi)
            if store:
                xs[rows, cre] = hr
                xs[rows, cim] = hi
            h_run[n] = (hr, hi)

    n_first = len(main_mm) // 2
    _interleave([(1.0, functools.partial(scan_step, i, False)) for i in range(SEG)], main_mm[:n_first])

    seg_start = [None] * len(lanes)
    sub = lax.broadcasted_iota(jnp.int32, (SUBLANES, SCAN_W), 0)
    for n, (cre, cim) in enumerate(lanes):
        er, ei = h_run[n]
        gr, gi = er, ei
        for k in range(3):
            gr, gi = _cmul_add(seg_ref[k, :, cre], seg_ref[k, :, cim],
                               pltpu.roll(gr, 1 << k, 0), pltpu.roll(gi, 1 << k, 0), gr, gi)
        gr = jnp.where(sub == 0, 0.0, pltpu.roll(gr, 1, 0))
        gi = jnp.where(sub == 0, 0.0, pltpu.roll(gi, 1, 0))
        gr, gi = _cmul_add(seg_ref[3, :, cre], seg_ref[3, :, cim], carry[:, cre], carry[:, cim], gr, gi)
        seg_start[n] = (gr, gi)

    h_run = seg_start
    _interleave([(1.0, functools.partial(scan_step, i, True)) for i in range(SEG)], main_mm[n_first:])
    for n, (cre, cim) in enumerate(lanes):
        hr, hi = h_run[n]
        carry[:, cre] = jnp.broadcast_to(hr[SUBLANES - 1:SUBLANES, :], (SUBLANES, SCAN_W))
        carry[:, cim] = jnp.broadcast_to(hi[SUBLANES - 1:SUBLANES, :], (SUBLANES, SCAN_W))

    decay = [None] * n_chunks

    def chunk_math(c):
        rows = slice(c * HG_CHUNK, (c + 1) * HG_CHUNK)
        b = fb[rows, :]
        b_mid = b[HG_CHUNK // 2 - 1:HG_CHUNK // 2, :]
        b_last = b[HG_CHUNK - 1:HG_CHUNK, :]
        qa = proj[rows, P_Q:P_Q + HG_WIDTH] * jnp.exp(b - b_mid)
        ka = kscr[rows, :] * jnp.exp(b_mid - b)
        qd_s[rows, :] = (qa * jnp.exp(b_mid)).astype(BF16)
        kd_s[rows, :] = (ka * jnp.exp(b_last - b_mid)).astype(BF16)
        qa_s[rows, :] = qa.astype(BF16)
        ka_s[rows, :] = ka.astype(BF16)
        iv_s[rows, :] = proj[rows, P_IV:P_IV + HG_WIDTH].astype(BF16)
        decay[c] = jnp.exp(b_last)

    def s5_out_a():
        y_parts = [_dot(xs[:, hf * half_st:(hf + 1) * half_st].astype(BF16), w_c_ref[hf]) for hf in range(2)]
        y_p = jnp.concatenate(y_parts, axis=1)
        y_hi = y_p.astype(BF16)
        y_lo = (y_p - y_hi.astype(F32)).astype(BF16)
        y_s = _dot(perm_t_ref[...], y_hi) + _dot(perm_t_ref[...], y_lo) + s5_d_ref[...] * s5p[:, 0:S5_WIDTH]
        xs[:, 0:S5_WIDTH] = jax.nn.gelu(y_s)

    def s5_out_b():
        glu = _dot(xs[:, 0:S5_WIDTH].astype(BF16), w_glu_ref[...]) + b_glu_ref[...]
        z_s = s5p[:, S5_WIDTH:2 * S5_WIDTH]
        ys = glu[:, 0:S5_WIDTH] * jax.nn.sigmoid(glu[:, S5_WIDTH:2 * S5_WIDTH]) * (z_s * jax.nn.sigmoid(z_s))
        xs[:, D_MODEL:2 * D_MODEL] = _dot(ys.astype(BF16), w_o_s5_ref[...])

    def ple():
        pe_s[...] = _dot(p_ref[...].astype(BF16), w_ple_ref[...])

    _interleave([(1.0, functools.partial(chunk_math, c)) for c in range(n_chunks)],
                [(2.0, s5_out_a), (2.0, s5_out_b), (0.5, ple)])

    row = lax.broadcasted_iota(jnp.int32, (ts, ts), 0)
    col = lax.broadcasted_iota(jnp.int32, (ts, ts), 1)
    causal = (col <= row) & (col >= (row & -HG_CHUNK))

    for h in range(HG_HEADS):
        sl = slice(h * HG_KEY, (h + 1) * HG_KEY)
        scores = jnp.where(causal, _dot_nt(qa_s[:, sl], ka_s[:, sl]), 0.0)
        o_scr[:, sl] = _dot(scores.astype(BF16), iv_s[:, sl])
    for c in range(n_chunks):
        rows = slice(c * HG_CHUNK, (c + 1) * HG_CHUNK)
        for h in range(HG_HEADS):
            sl = slice(h * HG_KEY, (h + 1) * HG_KEY)
            st = state[h]
            o = o_scr[rows, sl] + _dot_nt(qd_s[rows, sl], st.astype(BF16))
            state[h] = st * decay[c][:, sl] + _dot_tn(iv_s[rows, sl], kd_s[rows, sl])
            o_scr[rows, sl] = _rms(o)

    g_hg = proj[:, P_G:P_G + HG_WIDTH]
    o_act = o_scr[...] * hg_norm_g_ref[...] * (g_hg * jax.nn.sigmoid(g_hg))
    y_hg = _dot(o_act.astype(BF16), w_o_hg_ref[...])
    merged = (jax.nn.sigmoid(proj[:, P_GATE_HG:P_GATE_HG + D_MODEL]) * y_hg
              + jax.nn.sigmoid(proj[:, P_GATE_S5:P_GATE_S5 + D_MODEL]) * xs[:, D_MODEL:2 * D_MODEL])
    h = x_ref[...] + _dot(merged.astype(BF16), w_out_ref[...])
    gate = jax.nn.sigmoid(_dot((_rms(h) * ple_norm_g_ref[...]).astype(BF16), w_ple_gate_ref[...]))
    h = h + pe_s[...] * gate
    out_ref[...] = _rms(h) * final_norm_g_ref[...]


def _s5_tables(a_re, a_im, log_dt, b_re, b_im, c_re, c_im):
    g, n, c = S5_GROUPS, S5_STATE, S5_GROUP
    dt = jnp.exp(log_dt.astype(F32))[:, None]
    ar, ai = a_re.astype(F32), a_im.astype(F32)
    mag = jnp.exp(ar * dt)
    lr, li = mag * jnp.cos(ai * dt), mag * jnp.sin(ai * dt)
    den = ar * ar + ai * ai
    nr = lr - 1.0
    sr = (nr * ar + li * ai) / den
    si = (li * ar - nr * ai) / den
    br, bi = b_re.astype(F32), b_im.astype(F32)
    bbr = sr[..., None] * br - si[..., None] * bi
    bbi = sr[..., None] * bi + si[..., None] * br
    eye = jnp.eye(g, dtype=F32)
    hs, hi_ = S5_STATES // 2, S5_WIDTH // 2

    def in_mat(bb):
        return jnp.einsum('gnc,gh->gchn', bb, eye).reshape(g * c, g * n)

    def out_mat(cc):
        return jnp.einsum('gcn,gh->gnhc', cc.astype(F32), eye).reshape(g * n, g * c)

    mbr, mbi = in_mat(bbr), in_mat(bbi)
    w_b = jnp.stack([jnp.concatenate([mbr[k * hi_:(k + 1) * hi_, k * hs:(k + 1) * hs],
                                      mbi[k * hi_:(k + 1) * hi_, k * hs:(k + 1) * hs]], axis=1)
                     for k in range(2)]).astype(BF16)
    mcr, mci = out_mat(c_re), out_mat(c_im)
    w_c = jnp.stack([jnp.concatenate([mcr[k * hs:(k + 1) * hs, k * hi_:(k + 1) * hi_],
                                      -mci[k * hs:(k + 1) * hs, k * hi_:(k + 1) * hi_]], axis=0)
                     for k in range(2)]).astype(BF16)

    def layout(re, im):
        return jnp.concatenate([re[:, :hs], im[:, :hs], re[:, hs:], im[:, hs:]], axis=1)

    def powers(re, im, count):
        pr, pi = re[None, :], im[None, :]
        while pr.shape[0] < count:
            tr, ti = pr[-1:], pi[-1:]
            pr, pi = (jnp.concatenate([pr, pr * tr - pi * ti]), jnp.concatenate([pi, pr * ti + pi * tr]))
        return pr, pi

    lr, li = lr.reshape(-1), li.reshape(-1)
    pr, pi = powers(lr, li, SEG)
    sr, si = powers(pr[-1], pi[-1], SUBLANES)
    sp = layout(jnp.concatenate([jnp.ones_like(sr[:1]), sr[:-1]]),
                jnp.concatenate([jnp.zeros_like(si[:1]), si[:-1]]))
    rows = jnp.arange(SUBLANES)[:, None]
    bcast = lambda row: jnp.broadcast_to(row[None, :], (SUBLANES, 2 * S5_STATES))
    lam = bcast(layout(lr[None, :], li[None, :])[0])
    seg = jnp.stack([jnp.where(rows >= k, bcast(sp[k]), 0.0) for k in (1, 2, 4)] + [sp])
    return w_b, w_c, lam, seg


def kernel(x, p, norm_g, w_in, hg_lb, hg_norm_g, w_o_hg, s5_a_re, s5_a_im, s5_log_dt, s5_b_re, s5_b_im,
           s5_c_re, s5_c_im, s5_d, w_glu, b_glu, w_o_s5, w_out, ple_norm_g, w_ple, w_ple_gate, final_norm_g):
    bsz, seq, d = x.shape
    depth = w_in.shape[0]
    assert depth == 1 and d == D_MODEL and seq % TS == 0 and TS % HG_CHUNK == 0
    l = 0
    lb = jnp.cumsum(jax.nn.softmax(hg_lb.astype(F32), axis=0), axis=0)[l].reshape(1, HG_WIDTH)
    w_b, w_c, lam, seg = _s5_tables(s5_a_re[l], s5_a_im[l], s5_log_dt[l], s5_b_re[l], s5_b_im[l],
                                        s5_c_re[l], s5_c_im[l])
    r = jnp.arange(TS)
    tri = ((r[:, None] >= r[None, :]) & (r[:, None] // HG_CHUNK == r[None, :] // HG_CHUNK)).astype(BF16)
    perm = (r[None, :] == (r[:, None] % SUBLANES) * SEG + r[:, None] // SUBLANES).astype(BF16)

    w_in_l = w_in[l].astype(BF16)
    q_c, f_c, i_c, g_c, s5_c, gate_c = (slice(a, b) for a, b in (
        (0, 1024), (1024, 2048), (2048, 3072), (3072, 4096), (4096, 5120), (5120, 7168)))
    w_in_k = jnp.concatenate([w_in_l[:, c] for c in (f_c, s5_c, q_c, i_c, g_c, gate_c)], axis=1)
    weights = [w_in_k, w_o_hg[l].astype(BF16), w_glu[l].astype(BF16), w_o_s5[l].astype(BF16),
               w_out[l].astype(BF16), w_ple[l].astype(BF16), w_ple_gate[l].astype(BF16), w_b, w_c, tri,
               perm, perm.T, lam, seg]
    rows_ = [norm_g[l].reshape(1, d), lb, hg_norm_g[l].reshape(1, HG_WIDTH), b_glu[l].reshape(1, 2 * S5_WIDTH),
             s5_d[l].reshape(1, S5_WIDTH), ple_norm_g[l].reshape(1, d), final_norm_g.reshape(1, d)]
    rows_ = [a.astype(F32) for a in rows_]

    def resident(a):
        nd = a.ndim
        return pl.BlockSpec(a.shape, lambda b, s, _nd=nd: (0,) * _nd, pipeline_mode=pl.Buffered(1))

    in_specs = ([pl.BlockSpec((None, TS, d), lambda b, s: (b, s, 0)),
                 pl.BlockSpec((None, TS, PLE_DIM), lambda b, s: (b, s, 0))]
                + [resident(a) for a in weights] + [resident(a) for a in rows_])
    scratch = [
        pltpu.VMEM((TS, d), BF16),
        pltpu.VMEM((TS, HG_WIDTH), F32),
        pltpu.VMEM((TS, MAIN_COLS), F32),
        pltpu.VMEM((TS, HG_WIDTH), F32),
        pltpu.VMEM((TS, HG_WIDTH), F32),
        pltpu.VMEM((TS, 2 * S5_WIDTH), F32),
        pltpu.VMEM((TS, 2 * S5_STATES), F32),
        pltpu.VMEM((HG_HEADS, HG_VAL, HG_KEY), F32),
        pltpu.VMEM((SUBLANES, 2 * S5_STATES), F32),
        pltpu.VMEM((TS, D_MODEL), F32),
    ] + [pltpu.VMEM((TS, HG_WIDTH), BF16)] * 5
    return pl.pallas_call(
        _block_kernel,
        out_shape=jax.ShapeDtypeStruct((bsz, seq, d), x.dtype),
        grid=(bsz, seq // TS),
        in_specs=in_specs,
        out_specs=pl.BlockSpec((None, TS, d), lambda b, s: (b, s, 0)),
        scratch_shapes=scratch,
        compiler_params=pltpu.CompilerParams(dimension_semantics=("arbitrary", "arbitrary"),
                                             vmem_limit_bytes=VMEM_LIMIT_BYTES),
        name="hgrn2_s5_block",
    )(x, p[l], *weights, *rows_)
```

```python
import functools

import jax
import jax.numpy as jnp
from jax import lax
from jax.experimental import pallas as pl
from jax.experimental.pallas import tpu as pltpu

D_MODEL = 1024
PLE_DIM = 256
HG_HEADS = 8
HG_KEY = 128
HG_VAL = 128
HG_WIDTH = HG_HEADS * HG_VAL
HG_CHUNK = 64
S5_GROUP = 16
S5_WIDTH = 512
S5_GROUPS = S5_WIDTH // S5_GROUP
S5_STATE = 64
S5_STATES = S5_GROUPS * S5_STATE
NORM_EPS = 1e-6

COL_F = 0
COL_S5 = HG_WIDTH
COL_MAIN = COL_S5 + 2 * S5_WIDTH
P_Q, P_IV, P_G, P_GATE_HG, P_GATE_S5 = (k * HG_WIDTH for k in range(5))
MAIN_COLS = 5 * HG_WIDTH

SUBLANES = 8
TS = 256
SEG = TS // SUBLANES
SCAN_W = 512
MM_PANEL = 256
VMEM_LIMIT_BYTES = 58 * 1024 * 1024

F32 = jnp.float32
BF16 = jnp.bfloat16


def _dot(a, b):
    return jnp.dot(a, b, preferred_element_type=F32)


def _dot_nt(a, b):
    return lax.dot_general(a, b, (((1,), (1,)), ((), ())), preferred_element_type=F32)


def _dot_tn(a, b):
    return lax.dot_general(a, b, (((0,), (0,)), ((), ())), preferred_element_type=F32)


def _rms(x):
    return x * lax.rsqrt(jnp.mean(x * x, axis=-1, keepdims=True) + NORM_EPS)


def _cmul_add(ar, ai, br, bi, cr, ci):
    return ar * br - ai * bi + cr, ar * bi + ai * br + ci


def _interleave(*streams):
    totals = [sum(c for c, _ in s) or 1.0 for s in streams]
    pos = [0] * len(streams)
    done = [0.0] * len(streams)
    while True:
        live = [i for i, s in enumerate(streams) if pos[i] < len(s)]
        if not live:
            return
        i = min(live, key=lambda k: done[k] / totals[k])
        cost, thunk = streams[i][pos[i]]
        thunk()
        pos[i] += 1
        done[i] += cost


def _block_kernel(x_ref, p_ref, w_in_ref, w_o_hg_ref, w_glu_ref, w_o_s5_ref, w_out_ref, w_ple_ref,
                  w_ple_gate_ref, w_b_ref, w_c_ref, tri_ref, perm_ref, perm_t_ref, lam_ref, seg_ref,
                  norm_g_ref, lb_ref, hg_norm_g_ref, b_glu_ref, s5_d_ref, ple_norm_g_ref, final_norm_g_ref,
                  out_ref,
                  u_bf, fb, proj, kscr, o_scr, s5p, xs, state, carry, pe_s, qa_s, ka_s, qd_s, kd_s, iv_s):
    ts = x_ref.shape[0]
    n_chunks = ts // HG_CHUNK
    half_in = S5_WIDTH // 2
    half_st = 2 * (S5_STATES // 2)
    n_re = S5_STATES // 2

    @pl.when(pl.program_id(1) == 0)
    def _():
        state[...] = jnp.zeros_like(state)
        carry[...] = jnp.zeros_like(carry)

    def in_proj(dst, dst_col, src_col, n_cols):
        out = []
        for j in range(0, n_cols, MM_PANEL):
            def thunk(j=j):
                dst[:, dst_col + j:dst_col + j + MM_PANEL] = _dot(
                    u_bf[...], w_in_ref[:, src_col + j:src_col + j + MM_PANEL])
            out.append((MM_PANEL * D_MODEL, thunk))
        return out

    u_bf[...] = (_rms(x_ref[...]) * norm_g_ref[...]).astype(BF16)
    for _, thunk in in_proj(fb, 0, COL_F, HG_WIDTH):
        thunk()

    def gate_math(c):
        rows = slice(c * HG_CHUNK, (c + 1) * HG_CHUNK)
        lb = lb_ref[...]
        sig = jax.nn.sigmoid(fb[rows, :])
        kscr[rows, :] = (1.0 - lb) * (1.0 - sig)
        logf = jnp.log(lb + (1.0 - lb) * sig)
        logf_hi = logf.astype(BF16)
        qa_s[rows, :] = logf_hi
        ka_s[rows, :] = (logf - logf_hi.astype(F32)).astype(BF16)

    def s5_perm():
        qd_s[:, 0:S5_WIDTH] = _dot(perm_ref[...], s5p[:, 0:S5_WIDTH].astype(BF16)).astype(BF16)

    lanes = [(hf, slice(hf * half_st + lc, hf * half_st + lc + SCAN_W),
              slice(hf * half_st + n_re + lc, hf * half_st + n_re + lc + SCAN_W), lc)
             for hf in range(2) for lc in range(0, n_re, SCAN_W)]

    def s5_in(n):
        hf, cre, cim, lc = lanes[n]
        u_s = qd_s[:, hf * half_in:(hf + 1) * half_in]
        xs[:, cre] = _dot(u_s, w_b_ref[hf, :, lc:lc + SCAN_W])
        xs[:, cim] = _dot(u_s, w_b_ref[hf, :, n_re + lc:n_re + lc + SCAN_W])

    _interleave([(1.0, functools.partial(gate_math, c)) for c in range(n_chunks)],
                in_proj(s5p, 0, COL_S5, 2 * S5_WIDTH) + [(ts * S5_WIDTH, s5_perm),
                                                         (2 * SCAN_W * half_in, functools.partial(s5_in, 0))])

    def cumsum(j):
        cols = slice(j, j + MM_PANEL)
        fb[:, cols] = _dot(tri_ref[...], qa_s[:, cols]) + _dot(tri_ref[...], ka_s[:, cols])

    main_mm = ([(2 * MM_PANEL * ts, functools.partial(cumsum, j)) for j in range(0, HG_WIDTH, MM_PANEL)]
               + in_proj(proj, 0, COL_MAIN, MAIN_COLS))

    sub = lax.broadcasted_iota(jnp.int32, (SUBLANES, SCAN_W), 0)
    y_acc = [None, None]

    def scan_items(n):
        _, cre, cim, _ = lanes[n]
        run = [None]

        def step(i, store):
            rows = slice(i * SUBLANES, (i + 1) * SUBLANES)
            hr, hi = xs[rows, cre], xs[rows, cim]
            if run[0] is not None:
                hr, hi = _cmul_add(lam_ref[:, cre], lam_ref[:, cim], *run[0], hr, hi)
            if store:
                xs[rows, cre] = hr
                xs[rows, cim] = hi
            run[0] = (hr, hi)

        def seg_starts():
            gr, gi = run[0]
            for k in range(3):
                gr, gi = _cmul_add(seg_ref[k, :, cre], seg_ref[k, :, cim],
                                   pltpu.roll(gr, 1 << k, 0), pltpu.roll(gi, 1 << k, 0), gr, gi)
            gr = jnp.where(sub == 0, 0.0, pltpu.roll(gr, 1, 0))
            gi = jnp.where(sub == 0, 0.0, pltpu.roll(gi, 1, 0))
            run[0] = _cmul_add(seg_ref[3, :, cre], seg_ref[3, :, cim], carry[:, cre], carry[:, cim], gr, gi)

        def carry_out():
            hr, hi = run[0]
            carry[:, cre] = jnp.broadcast_to(hr[SUBLANES - 1:SUBLANES, :], (SUBLANES, SCAN_W))
            carry[:, cim] = jnp.broadcast_to(hi[SUBLANES - 1:SUBLANES, :], (SUBLANES, SCAN_W))

        return ([(1.0, functools.partial(step, i, False)) for i in range(SEG)] + [(1.0, seg_starts)]
                + [(1.0, functools.partial(step, i, True)) for i in range(SEG)] + [(0.1, carry_out)])

    def s5_out_part(n):
        hf, cre, cim, lc = lanes[n]
        part = (_dot(xs[:, cre].astype(BF16), w_c_ref[hf, lc:lc + SCAN_W, :])
                + _dot(xs[:, cim].astype(BF16), w_c_ref[hf, n_re + lc:n_re + lc + SCAN_W, :]))
        y_acc[hf] = part if y_acc[hf] is None else y_acc[hf] + part

    share = -(-len(main_mm) // len(lanes))
    for n in range(len(lanes)):
        beside = []
        if n + 1 < len(lanes):
            beside.append((2 * SCAN_W * half_in, functools.partial(s5_in, n + 1)))
        if n > 0:
            beside.append((2 * SCAN_W * half_in, functools.partial(s5_out_part, n - 1)))
        _interleave(scan_items(n), beside + main_mm[n * share:(n + 1) * share])
    s5_out_part(len(lanes) - 1)

    decay = [None] * n_chunks

    def chunk_math(c):
        rows = slice(c * HG_CHUNK, (c + 1) * HG_CHUNK)
        b = fb[rows, :]
        b_mid = b[HG_CHUNK // 2 - 1:HG_CHUNK // 2, :]
        b_last = b[HG_CHUNK - 1:HG_CHUNK, :]
        qa = proj[rows, P_Q:P_Q + HG_WIDTH] * jnp.exp(b - b_mid)
        ka = kscr[rows, :] * jnp.exp(b_mid - b)
        qd_s[rows, :] = (qa * jnp.exp(b_mid)).astype(BF16)
        kd_s[rows, :] = (ka * jnp.exp(b_last - b_mid)).astype(BF16)
        qa_s[rows, :] = qa.astype(BF16)
        ka_s[rows, :] = ka.astype(BF16)
        iv_s[rows, :] = proj[rows, P_IV:P_IV + HG_WIDTH].astype(BF16)
        decay[c] = jnp.exp(b_last)

    def s5_out_a():
        y_p = jnp.concatenate(y_acc, axis=1)
        y_hi = y_p.astype(BF16)
        y_lo = (y_p - y_hi.astype(F32)).astype(BF16)
        y_s = _dot(perm_t_ref[...], y_hi) + _dot(perm_t_ref[...], y_lo) + s5_d_ref[...] * s5p[:, 0:S5_WIDTH]
        xs[:, 0:S5_WIDTH] = jax.nn.gelu(y_s)

    def s5_out_b():
        glu = _dot(xs[:, 0:S5_WIDTH].astype(BF16), w_glu_ref[...]) + b_glu_ref[...]
        z_s = s5p[:, S5_WIDTH:2 * S5_WIDTH]
        ys = glu[:, 0:S5_WIDTH] * jax.nn.sigmoid(glu[:, S5_WIDTH:2 * S5_WIDTH]) * (z_s * jax.nn.sigmoid(z_s))
        xs[:, D_MODEL:2 * D_MODEL] = _dot(ys.astype(BF16), w_o_s5_ref[...])

    def ple():
        pe_s[...] = _dot(p_ref[...].astype(BF16), w_ple_ref[...])

    _interleave([(1.0, functools.partial(chunk_math, c)) for c in range(n_chunks)],
                [(2.0, s5_out_a), (2.0, s5_out_b), (0.5, ple)])

    row = lax.broadcasted_iota(jnp.int32, (ts, ts), 0)
    col = lax.broadcasted_iota(jnp.int32, (ts, ts), 1)
    causal = (col <= row) & (col >= (row & -HG_CHUNK))

    for h in range(HG_HEADS):
        sl = slice(h * HG_KEY, (h + 1) * HG_KEY)
        scores = jnp.where(causal, _dot_nt(qa_s[:, sl], ka_s[:, sl]), 0.0)
        o_scr[:, sl] = _dot(scores.astype(BF16), iv_s[:, sl])
    for c in range(n_chunks):
        rows = slice(c * HG_CHUNK, (c + 1) * HG_CHUNK)
        for h in range(HG_HEADS):
            sl = slice(h * HG_KEY, (h + 1) * HG_KEY)
            st = state[h]
            o = o_scr[rows, sl] + _dot_nt(qd_s[rows, sl], st.astype(BF16))
            state[h] = st * decay[c][:, sl] + _dot_tn(iv_s[rows, sl], kd_s[rows, sl])
            o_scr[rows, sl] = _rms(o)

    g_hg = proj[:, P_G:P_G + HG_WIDTH]
    o_act = o_scr[...] * hg_norm_g_ref[...] * (g_hg * jax.nn.sigmoid(g_hg))
    y_hg = _dot(o_act.astype(BF16), w_o_hg_ref[...])
    merged = (jax.nn.sigmoid(proj[:, P_GATE_HG:P_GATE_HG + D_MODEL]) * y_hg
              + jax.nn.sigmoid(proj[:, P_GATE_S5:P_GATE_S5 + D_MODEL]) * xs[:, D_MODEL:2 * D_MODEL])
    h = x_ref[...] + _dot(merged.astype(BF16), w_out_ref[...])
    gate = jax.nn.sigmoid(_dot((_rms(h) * ple_norm_g_ref[...]).astype(BF16), w_ple_gate_ref[...]))
    h = h + pe_s[...] * gate
    out_ref[...] = _rms(h) * final_norm_g_ref[...]


def _s5_tables(a_re, a_im, log_dt, b_re, b_im, c_re, c_im):
    g, n, c = S5_GROUPS, S5_STATE, S5_GROUP
    gh, hs, hi_ = g // 2, S5_STATES // 2, S5_WIDTH // 2
    hp = lax.Precision.HIGHEST
    dt = jnp.exp(log_dt.astype(F32))[:, None]
    ar, ai = a_re.astype(F32), a_im.astype(F32)

    m = jnp.asarray([1.0] + [float(SEG * j) for j in range(SUBLANES)], F32)[:, None, None]
    mag, ang = jnp.exp(m * (ar * dt)), m * (ai * dt)
    pr, pi = mag * jnp.cos(ang), mag * jnp.sin(ang)
    lr, li = pr[0], pi[0]
    den = ar * ar + ai * ai
    nr = lr - 1.0
    sr = (nr * ar + li * ai) / den
    si = (li * ar - nr * ai) / den
    br, bi = b_re.astype(F32), b_im.astype(F32)
    bbr = sr[..., None] * br - si[..., None] * bi
    bbi = sr[..., None] * bi + si[..., None] * br

    rows_c, cols_n = jnp.arange(hi_), jnp.arange(hs)
    a_in = jnp.stack([bbr, bbi]).reshape(2, 2, gh, n, c).transpose(1, 0, 2, 4, 3).reshape(2, 2, hi_, n)
    tile_n = (cols_n[None, :] % n == jnp.arange(n)[:, None]).astype(F32)
    t_in = jnp.einsum('hprn,nm->hprm', a_in, tile_n, precision=hp)
    t_in = jnp.where(rows_c[:, None] // c == cols_n[None, :] // n, t_in, 0.0)
    w_b = jnp.concatenate([t_in[:, 0], t_in[:, 1]], axis=-1).astype(BF16)

    a_out = jnp.stack([c_re.astype(F32), -c_im.astype(F32)]).reshape(2, 2, gh, c, n)
    a_out = a_out.transpose(1, 0, 2, 4, 3).reshape(2, 2, hs, c)
    tile_c = (rows_c[None, :] % c == jnp.arange(c)[:, None]).astype(F32)
    t_out = jnp.einsum('hprc,cm->hprm', a_out, tile_c, precision=hp)
    t_out = jnp.where(cols_n[:, None] // n == rows_c[None, :] // c, t_out, 0.0)
    w_c = t_out.reshape(2, 2 * hs, hi_).astype(BF16)

    pr, pi = pr.reshape(-1, S5_STATES), pi.reshape(-1, S5_STATES)
    pw = jnp.concatenate([pr[:, :hs], pi[:, :hs], pr[:, hs:], pi[:, hs:]], axis=1)
    rows = jnp.arange(SUBLANES)[:, None]
    bcast = lambda row: jnp.broadcast_to(row[None, :], (SUBLANES, 2 * S5_STATES))
    lam = bcast(pw[0])
    sp = pw[1:]
    seg = jnp.stack([jnp.where(rows >= k, bcast(sp[k]), 0.0) for k in (1, 2, 4)] + [sp])
    return w_b, w_c, lam, seg


def kernel(x, p, norm_g, w_in, hg_lb, hg_norm_g, w_o_hg, s5_a_re, s5_a_im, s5_log_dt, s5_b_re, s5_b_im,
           s5_c_re, s5_c_im, s5_d, w_glu, b_glu, w_o_s5, w_out, ple_norm_g, w_ple, w_ple_gate, final_norm_g):
    bsz, seq, d = x.shape
    depth = w_in.shape[0]
    assert depth == 1 and d == D_MODEL and seq % TS == 0 and TS % HG_CHUNK == 0
    l = 0
    lb = jnp.cumsum(jax.nn.softmax(hg_lb.astype(F32), axis=0), axis=0)[l].reshape(1, HG_WIDTH)
    w_b, w_c, lam, seg = _s5_tables(s5_a_re[l], s5_a_im[l], s5_log_dt[l], s5_b_re[l], s5_b_im[l],
                                        s5_c_re[l], s5_c_im[l])
    r = jnp.arange(TS)
    tri = ((r[:, None] >= r[None, :]) & (r[:, None] // HG_CHUNK == r[None, :] // HG_CHUNK)).astype(BF16)
    perm = (r[None, :] == (r[:, None] % SUBLANES) * SEG + r[:, None] // SUBLANES).astype(BF16)

    w_in_l = w_in[l].astype(BF16)
    q_c, f_c, i_c, g_c, s5_c, gate_c = (slice(a, b) for a, b in (
        (0, 1024), (1024, 2048), (2048, 3072), (3072, 4096), (4096, 5120), (5120, 7168)))
    w_in_k = jnp.concatenate([w_in_l[:, c] for c in (f_c, s5_c, q_c, i_c, g_c, gate_c)], axis=1)
    weights = [w_in_k, w_o_hg[l].astype(BF16), w_glu[l].astype(BF16), w_o_s5[l].astype(BF16),
               w_out[l].astype(BF16), w_ple[l].astype(BF16), w_ple_gate[l].astype(BF16), w_b, w_c, tri,
               perm, perm.T, lam, seg]
    rows_ = [norm_g[l].reshape(1, d), lb, hg_norm_g[l].reshape(1, HG_WIDTH), b_glu[l].reshape(1, 2 * S5_WIDTH),
             s5_d[l].reshape(1, S5_WIDTH), ple_norm_g[l].reshape(1, d), final_norm_g.reshape(1, d)]
    rows_ = [a.astype(F32) for a in rows_]

    def resident(a):
        nd = a.ndim
        return pl.BlockSpec(a.shape, lambda b, s, _nd=nd: (0,) * _nd, pipeline_mode=pl.Buffered(1))

    in_specs = ([pl.BlockSpec((None, TS, d), lambda b, s: (b, s, 0)),
                 pl.BlockSpec((None, TS, PLE_DIM), lambda b, s: (b, s, 0))]
                + [resident(a) for a in weights] + [resident(a) for a in rows_])
    scratch = [
        pltpu.VMEM((TS, d), BF16),
        pltpu.VMEM((TS, HG_WIDTH), F32),
        pltpu.VMEM((TS, MAIN_COLS), F32),
        pltpu.VMEM((TS, HG_WIDTH), F32),
        pltpu.VMEM((TS, HG_WIDTH), F32),
        pltpu.VMEM((TS, 2 * S5_WIDTH), F32),
        pltpu.VMEM((TS, 2 * S5_STATES), F32),
        pltpu.VMEM((HG_HEADS, HG_VAL, HG_KEY), F32),
        pltpu.VMEM((SUBLANES, 2 * S5_STATES), F32),
        pltpu.VMEM((TS, D_MODEL), F32),
    ] + [pltpu.VMEM((TS, HG_WIDTH), BF16)] * 5
    return pl.pallas_call(
        _block_kernel,
        out_shape=jax.ShapeDtypeStruct((bsz, seq, d), x.dtype),
        grid=(bsz, seq // TS),
        in_specs=in_specs,
        out_specs=pl.BlockSpec((None, TS, d), lambda b, s: (b, s, 0)),
        scratch_shapes=scratch,
        compiler_params=pltpu.CompilerParams(dimension_semantics=("arbitrary", "arbitrary"),
                                             vmem_limit_bytes=VMEM_LIMIT_BYTES),
        name="hgrn2_s5_block",
    )(x, p[l], *weights, *rows_)
```

```python
import functools

import jax
import jax.numpy as jnp
from jax import lax
from jax.experimental import pallas as pl
from jax.experimental.pallas import tpu as pltpu

D_MODEL = 1024
PLE_DIM = 256
HG_HEADS = 8
HG_KEY = 128
HG_VAL = 128
HG_WIDTH = HG_HEADS * HG_VAL
HG_CHUNK = 64
S5_GROUP = 16
S5_WIDTH = 512
S5_GROUPS = S5_WIDTH // S5_GROUP
S5_STATE = 64
S5_STATES = S5_GROUPS * S5_STATE
NORM_EPS = 1e-6

COL_F = 0
COL_S5 = HG_WIDTH
COL_MAIN = COL_S5 + 2 * S5_WIDTH
P_Q, P_IV, P_G, P_GATE_HG, P_GATE_S5 = (k * HG_WIDTH for k in range(5))
MAIN_COLS = 5 * HG_WIDTH

SUBLANES = 8
TS = 512
SUBT = 256
SEG = SUBT // SUBLANES
SCAN_W = 512
MM_PANEL = 256
VMEM_LIMIT_BYTES = 62 * 1024 * 1024

F32 = jnp.float32
BF16 = jnp.bfloat16


def _dot(a, b):
    return jnp.dot(a, b, preferred_element_type=F32)


def _dot_nt(a, b):
    return lax.dot_general(a, b, (((1,), (1,)), ((), ())), preferred_element_type=F32)


def _dot_tn(a, b):
    return lax.dot_general(a, b, (((0,), (0,)), ((), ())), preferred_element_type=F32)


def _rms(x):
    return x * lax.rsqrt(jnp.mean(x * x, axis=-1, keepdims=True) + NORM_EPS)


def _cmul_add(ar, ai, br, bi, cr, ci):
    return ar * br - ai * bi + cr, ar * bi + ai * br + ci


def _interleave(*streams):
    totals = [sum(c for c, _ in s) or 1.0 for s in streams]
    pos = [0] * len(streams)
    done = [0.0] * len(streams)
    while True:
        live = [i for i, s in enumerate(streams) if pos[i] < len(s)]
        if not live:
            return
        i = min(live, key=lambda k: done[k] / totals[k])
        cost, thunk = streams[i][pos[i]]
        thunk()
        pos[i] += 1
        done[i] += cost


def _block_kernel(x_ref, p_ref, w_in_ref, w_o_hg_ref, w_glu_ref, w_o_s5_ref, w_out_ref, w_ple_ref,
                  w_ple_gate_ref, w_b_ref, w_c_ref, tri_ref, perm_ref, perm_t_ref, lam_ref, seg_ref,
                  norm_g_ref, lb_ref, hg_norm_g_ref, b_glu_ref, s5_d_ref, ple_norm_g_ref, final_norm_g_ref,
                  out_ref,
                  u_bf, fb, proj, kscr, o_scr, s5p, xs, state, carry, pe_s, qa_s, ka_s, qd_s, kd_s, iv_s):
    ts = x_ref.shape[0]
    n_chunks = ts // HG_CHUNK
    subs = [slice(r, r + SUBT) for r in range(0, ts, SUBT)]
    half_in = S5_WIDTH // 2
    half_st = 2 * (S5_STATES // 2)
    n_re = S5_STATES // 2

    @pl.when(pl.program_id(1) == 0)
    def _():
        state[...] = jnp.zeros_like(state)
        carry[...] = jnp.zeros_like(carry)

    def in_proj(dst, dst_col, src_col, n_cols):
        out = []
        for j in range(0, n_cols, MM_PANEL):
            def thunk(j=j):
                dst[:, dst_col + j:dst_col + j + MM_PANEL] = _dot(
                    u_bf[...], w_in_ref[:, src_col + j:src_col + j + MM_PANEL])
            out.append((MM_PANEL * D_MODEL, thunk))
        return out

    u_bf[...] = (_rms(x_ref[...]) * norm_g_ref[...]).astype(BF16)
    for _, thunk in in_proj(fb, 0, COL_F, HG_WIDTH):
        thunk()

    def gate_math(c):
        rows = slice(c * HG_CHUNK, (c + 1) * HG_CHUNK)
        lb = lb_ref[...]
        sig = jax.nn.sigmoid(fb[rows, :])
        kscr[rows, :] = (1.0 - lb) * (1.0 - sig)
        logf = jnp.log(lb + (1.0 - lb) * sig)
        logf_hi = logf.astype(BF16)
        qa_s[rows, :] = logf_hi
        ka_s[rows, :] = (logf - logf_hi.astype(F32)).astype(BF16)

    def s5_perm():
        for rs in subs:
            qd_s[rs, 0:S5_WIDTH] = _dot(perm_ref[...], s5p[rs, 0:S5_WIDTH].astype(BF16)).astype(BF16)

    lanes = [(hf, slice(hf * half_st + lc, hf * half_st + lc + SCAN_W),
              slice(hf * half_st + n_re + lc, hf * half_st + n_re + lc + SCAN_W), lc)
             for hf in range(2) for lc in range(0, n_re, SCAN_W)]

    def s5_in(n):
        hf, cre, cim, lc = lanes[n]
        u_s = qd_s[:, hf * half_in:(hf + 1) * half_in]
        xs[:, cre] = _dot(u_s, w_b_ref[hf, :, lc:lc + SCAN_W])
        xs[:, cim] = _dot(u_s, w_b_ref[hf, :, n_re + lc:n_re + lc + SCAN_W])

    _interleave([(1.0, functools.partial(gate_math, c)) for c in range(n_chunks)],
                in_proj(s5p, 0, COL_S5, 2 * S5_WIDTH) + [(ts * S5_WIDTH, s5_perm),
                                                         (2 * SCAN_W * half_in, functools.partial(s5_in, 0))])

    def cumsum(rs, j):
        cols = slice(j, j + MM_PANEL)
        fb[rs, cols] = _dot(tri_ref[...], qa_s[rs, cols]) + _dot(tri_ref[...], ka_s[rs, cols])

    main_mm = ([(2 * MM_PANEL * SUBT, functools.partial(cumsum, rs, j))
                for rs in subs for j in range(0, HG_WIDTH, MM_PANEL)]
               + in_proj(proj, 0, COL_MAIN, MAIN_COLS))

    sub = lax.broadcasted_iota(jnp.int32, (SUBLANES, SCAN_W), 0)
    y_acc = [None, None]

    def scan_items(n, rs):
        _, cre, cim, _ = lanes[n]
        run = [None]

        def step(i, store):
            rows = slice(rs.start + i * SUBLANES, rs.start + (i + 1) * SUBLANES)
            hr, hi = xs[rows, cre], xs[rows, cim]
            if run[0] is not None:
                hr, hi = _cmul_add(lam_ref[:, cre], lam_ref[:, cim], *run[0], hr, hi)
            if store:
                xs[rows, cre] = hr
                xs[rows, cim] = hi
            run[0] = (hr, hi)

        def seg_starts():
            gr, gi = run[0]
            for k in range(3):
                gr, gi = _cmul_add(seg_ref[k, :, cre], seg_ref[k, :, cim],
                                   pltpu.roll(gr, 1 << k, 0), pltpu.roll(gi, 1 << k, 0), gr, gi)
            gr = jnp.where(sub == 0, 0.0, pltpu.roll(gr, 1, 0))
            gi = jnp.where(sub == 0, 0.0, pltpu.roll(gi, 1, 0))
            run[0] = _cmul_add(seg_ref[3, :, cre], seg_ref[3, :, cim], carry[:, cre], carry[:, cim], gr, gi)

        def carry_out():
            hr, hi = run[0]
            carry[:, cre] = jnp.broadcast_to(hr[SUBLANES - 1:SUBLANES, :], (SUBLANES, SCAN_W))
            carry[:, cim] = jnp.broadcast_to(hi[SUBLANES - 1:SUBLANES, :], (SUBLANES, SCAN_W))

        return ([(1.0, functools.partial(step, i, False)) for i in range(SEG)] + [(1.0, seg_starts)]
                + [(1.0, functools.partial(step, i, True)) for i in range(SEG)] + [(0.1, carry_out)])

    def s5_out_part(n):
        hf, cre, cim, lc = lanes[n]
        part = (_dot(xs[:, cre].astype(BF16), w_c_ref[hf, lc:lc + SCAN_W, :])
                + _dot(xs[:, cim].astype(BF16), w_c_ref[hf, n_re + lc:n_re + lc + SCAN_W, :]))
        y_acc[hf] = part if y_acc[hf] is None else y_acc[hf] + part

    share = -(-len(main_mm) // len(lanes))
    for n in range(len(lanes)):
        beside = []
        if n + 1 < len(lanes):
            beside.append((2 * SCAN_W * half_in, functools.partial(s5_in, n + 1)))
        if n > 0:
            beside.append((2 * SCAN_W * half_in, functools.partial(s5_out_part, n - 1)))
        _interleave([item for rs in subs for item in scan_items(n, rs)],
                    beside + main_mm[n * share:(n + 1) * share])
    s5_out_part(len(lanes) - 1)

    decay = [None] * n_chunks

    def chunk_math(c):
        rows = slice(c * HG_CHUNK, (c + 1) * HG_CHUNK)
        b = fb[rows, :]
        b_mid = b[HG_CHUNK // 2 - 1:HG_CHUNK // 2, :]
        b_last = b[HG_CHUNK - 1:HG_CHUNK, :]
        qa = proj[rows, P_Q:P_Q + HG_WIDTH] * jnp.exp(b - b_mid)
        ka = kscr[rows, :] * jnp.exp(b_mid - b)
        qd_s[rows, :] = (qa * jnp.exp(b_mid)).astype(BF16)
        kd_s[rows, :] = (ka * jnp.exp(b_last - b_mid)).astype(BF16)
        qa_s[rows, :] = qa.astype(BF16)
        ka_s[rows, :] = ka.astype(BF16)
        iv_s[rows, :] = proj[rows, P_IV:P_IV + HG_WIDTH].astype(BF16)
        decay[c] = jnp.exp(b_last)

    def s5_out_a():
        y_p = jnp.concatenate(y_acc, axis=1)
        y_hi = y_p.astype(BF16)
        y_lo = (y_p - y_hi.astype(F32)).astype(BF16)
        for rs in subs:
            y_s = (_dot(perm_t_ref[...], y_hi[rs, :]) + _dot(perm_t_ref[...], y_lo[rs, :])
                   + s5_d_ref[...] * s5p[rs, 0:S5_WIDTH])
            xs[rs, 0:S5_WIDTH] = jax.nn.gelu(y_s)

    def s5_out_b():
        glu = _dot(xs[:, 0:S5_WIDTH].astype(BF16), w_glu_ref[...]) + b_glu_ref[...]
        z_s = s5p[:, S5_WIDTH:2 * S5_WIDTH]
        ys = glu[:, 0:S5_WIDTH] * jax.nn.sigmoid(glu[:, S5_WIDTH:2 * S5_WIDTH]) * (z_s * jax.nn.sigmoid(z_s))
        xs[:, D_MODEL:2 * D_MODEL] = _dot(ys.astype(BF16), w_o_s5_ref[...])

    def ple():
        pe_s[...] = _dot(p_ref[...].astype(BF16), w_ple_ref[...])

    _interleave([(1.0, functools.partial(chunk_math, c)) for c in range(n_chunks)],
                [(2.0, s5_out_a), (2.0, s5_out_b), (0.5, ple)])

    row = lax.broadcasted_iota(jnp.int32, (SUBT, SUBT), 0)
    col = lax.broadcasted_iota(jnp.int32, (SUBT, SUBT), 1)
    causal = (col <= row) & (col >= (row & -HG_CHUNK))

    for rs in subs:
        for h in range(HG_HEADS):
            sl = slice(h * HG_KEY, (h + 1) * HG_KEY)
            scores = jnp.where(causal, _dot_nt(qa_s[rs, sl], ka_s[rs, sl]), 0.0)
            o_scr[rs, sl] = _dot(scores.astype(BF16), iv_s[rs, sl])
    for c in range(n_chunks):
        rows = slice(c * HG_CHUNK, (c + 1) * HG_CHUNK)
        for h in range(HG_HEADS):
            sl = slice(h * HG_KEY, (h + 1) * HG_KEY)
            st = state[h]
            o = o_scr[rows, sl] + _dot_nt(qd_s[rows, sl], st.astype(BF16))
            state[h] = st * decay[c][:, sl] + _dot_tn(iv_s[rows, sl], kd_s[rows, sl])
            o_scr[rows, sl] = _rms(o)

    g_hg = proj[:, P_G:P_G + HG_WIDTH]
    o_act = o_scr[...] * hg_norm_g_ref[...] * (g_hg * jax.nn.sigmoid(g_hg))
    y_hg = _dot(o_act.astype(BF16), w_o_hg_ref[...])
    merged = (jax.nn.sigmoid(proj[:, P_GATE_HG:P_GATE_HG + D_MODEL]) * y_hg
              + jax.nn.sigmoid(proj[:, P_GATE_S5:P_GATE_S5 + D_MODEL]) * xs[:, D_MODEL:2 * D_MODEL])
    h = x_ref[...] + _dot(merged.astype(BF16), w_out_ref[...])
    gate = jax.nn.sigmoid(_dot((_rms(h) * ple_norm_g_ref[...]).astype(BF16), w_ple_gate_ref[...]))
    h = h + pe_s[...] * gate
    out_ref[...] = _rms(h) * final_norm_g_ref[...]


def _s5_tables(a_re, a_im, log_dt, b_re, b_im, c_re, c_im):
    g, n, c = S5_GROUPS, S5_STATE, S5_GROUP
    gh, hs, hi_ = g // 2, S5_STATES // 2, S5_WIDTH // 2
    hp = lax.Precision.HIGHEST
    dt = jnp.exp(log_dt.astype(F32))[:, None]
    ar, ai = a_re.astype(F32), a_im.astype(F32)

    m = jnp.asarray([1.0] + [float(SEG * j) for j in range(SUBLANES)], F32)[:, None, None]
    mag, ang = jnp.exp(m * (ar * dt)), m * (ai * dt)
    pr, pi = mag * jnp.cos(ang), mag * jnp.sin(ang)
    lr, li = pr[0], pi[0]
    den = ar * ar + ai * ai
    nr = lr - 1.0
    sr = (nr * ar + li * ai) / den
    si = (li * ar - nr * ai) / den
    br, bi = b_re.astype(F32), b_im.astype(F32)
    bbr = sr[..., None] * br - si[..., None] * bi
    bbi = sr[..., None] * bi + si[..., None] * br

    rows_c, cols_n = jnp.arange(hi_), jnp.arange(hs)
    a_in = jnp.stack([bbr, bbi]).reshape(2, 2, gh, n, c).transpose(1, 0, 2, 4, 3).reshape(2, 2, hi_, n)
    tile_n = (cols_n[None, :] % n == jnp.arange(n)[:, None]).astype(F32)
    t_in = jnp.einsum('hprn,nm->hprm', a_in, tile_n, precision=hp)
    t_in = jnp.where(rows_c[:, None] // c == cols_n[None, :] // n, t_in, 0.0)
    w_b = jnp.concatenate([t_in[:, 0], t_in[:, 1]], axis=-1).astype(BF16)

    a_out = jnp.stack([c_re.astype(F32), -c_im.astype(F32)]).reshape(2, 2, gh, c, n)
    a_out = a_out.transpose(1, 0, 2, 4, 3).reshape(2, 2, hs, c)
    tile_c = (rows_c[None, :] % c == jnp.arange(c)[:, None]).astype(F32)
    t_out = jnp.einsum('hprc,cm->hprm', a_out, tile_c, precision=hp)
    t_out = jnp.where(cols_n[:, None] // n == rows_c[None, :] // c, t_out, 0.0)
    w_c = t_out.reshape(2, 2 * hs, hi_).astype(BF16)

    pr, pi = pr.reshape(-1, S5_STATES), pi.reshape(-1, S5_STATES)
    pw = jnp.concatenate([pr[:, :hs], pi[:, :hs], pr[:, hs:], pi[:, hs:]], axis=1)
    rows = jnp.arange(SUBLANES)[:, None]
    bcast = lambda row: jnp.broadcast_to(row[None, :], (SUBLANES, 2 * S5_STATES))
    lam = bcast(pw[0])
    sp = pw[1:]
    seg = jnp.stack([jnp.where(rows >= k, bcast(sp[k]), 0.0) for k in (1, 2, 4)] + [sp])
    return w_b, w_c, lam, seg


def kernel(x, p, norm_g, w_in, hg_lb, hg_norm_g, w_o_hg, s5_a_re, s5_a_im, s5_log_dt, s5_b_re, s5_b_im,
           s5_c_re, s5_c_im, s5_d, w_glu, b_glu, w_o_s5, w_out, ple_norm_g, w_ple, w_ple_gate, final_norm_g):
    bsz, seq, d = x.shape
    depth = w_in.shape[0]
    assert depth == 1 and d == D_MODEL and seq % TS == 0 and TS % SUBT == 0 and SUBT % HG_CHUNK == 0
    l = 0
    lb = jnp.cumsum(jax.nn.softmax(hg_lb.astype(F32), axis=0), axis=0)[l].reshape(1, HG_WIDTH)
    w_b, w_c, lam, seg = _s5_tables(s5_a_re[l], s5_a_im[l], s5_log_dt[l], s5_b_re[l], s5_b_im[l],
                                        s5_c_re[l], s5_c_im[l])
    r = jnp.arange(SUBT)
    tri = ((r[:, None] >= r[None, :]) & (r[:, None] // HG_CHUNK == r[None, :] // HG_CHUNK)).astype(BF16)
    perm = (r[None, :] == (r[:, None] % SUBLANES) * SEG + r[:, None] // SUBLANES).astype(BF16)

    w_in_l = w_in[l].astype(BF16)
    q_c, f_c, i_c, g_c, s5_c, gate_c = (slice(a, b) for a, b in (
        (0, 1024), (1024, 2048), (2048, 3072), (3072, 4096), (4096, 5120), (5120, 7168)))
    w_in_k = jnp.concatenate([w_in_l[:, c] for c in (f_c, s5_c, q_c, i_c, g_c, gate_c)], axis=1)
    weights = [w_in_k, w_o_hg[l].astype(BF16), w_glu[l].astype(BF16), w_o_s5[l].astype(BF16),
               w_out[l].astype(BF16), w_ple[l].astype(BF16), w_ple_gate[l].astype(BF16), w_b, w_c, tri,
               perm, perm.T, lam, seg]
    rows_ = [norm_g[l].reshape(1, d), lb, hg_norm_g[l].reshape(1, HG_WIDTH), b_glu[l].reshape(1, 2 * S5_WIDTH),
             s5_d[l].reshape(1, S5_WIDTH), ple_norm_g[l].reshape(1, d), final_norm_g.reshape(1, d)]
    rows_ = [a.astype(F32) for a in rows_]

    def resident(a):
        nd = a.ndim
        return pl.BlockSpec(a.shape, lambda b, s, _nd=nd: (0,) * _nd, pipeline_mode=pl.Buffered(1))

    in_specs = ([pl.BlockSpec((None, TS, d), lambda b, s: (b, s, 0)),
                 pl.BlockSpec((None, TS, PLE_DIM), lambda b, s: (b, s, 0))]
                + [resident(a) for a in weights] + [resident(a) for a in rows_])
    scratch = [
        pltpu.VMEM((TS, d), BF16),
        pltpu.VMEM((TS, HG_WIDTH), F32),
        pltpu.VMEM((TS, MAIN_COLS), F32),
        pltpu.VMEM((TS, HG_WIDTH), F32),
        pltpu.VMEM((TS, HG_WIDTH), F32),
        pltpu.VMEM((TS, 2 * S5_WIDTH), F32),
        pltpu.VMEM((TS, 2 * S5_STATES), F32),
        pltpu.VMEM((HG_HEADS, HG_VAL, HG_KEY), F32),
        pltpu.VMEM((SUBLANES, 2 * S5_STATES), F32),
        pltpu.VMEM((TS, D_MODEL), F32),
    ] + [pltpu.VMEM((TS, HG_WIDTH), BF16)] * 5
    return pl.pallas_call(
        _block_kernel,
        out_shape=jax.ShapeDtypeStruct((bsz, seq, d), x.dtype),
        grid=(bsz, seq // TS),
        in_specs=in_specs,
        out_specs=pl.BlockSpec((None, TS, d), lambda b, s: (b, s, 0)),
        scratch_shapes=scratch,
        compiler_params=pltpu.CompilerParams(dimension_semantics=("arbitrary", "arbitrary"),
                                             vmem_limit_bytes=VMEM_LIMIT_BYTES),
        name="hgrn2_s5_block",
    )(x, p[l], *weights, *rows_)
```

```python
import functools

import jax
import jax.numpy as jnp
from jax import lax
from jax.experimental import pallas as pl
from jax.experimental.pallas import tpu as pltpu

D_MODEL = 1024
PLE_DIM = 256
HG_HEADS = 8
HG_KEY = 128
HG_VAL = 128
HG_WIDTH = HG_HEADS * HG_VAL
HG_CHUNK = 64
S5_GROUP = 16
S5_WIDTH = 512
S5_GROUPS = S5_WIDTH // S5_GROUP
S5_STATE = 64
S5_STATES = S5_GROUPS * S5_STATE
NORM_EPS = 1e-6

COL_F = 0
COL_S5 = HG_WIDTH
COL_MAIN = COL_S5 + 2 * S5_WIDTH
P_Q, P_IV, P_G, P_GATE_HG, P_GATE_S5 = (k * HG_WIDTH for k in range(5))
MAIN_COLS = 5 * HG_WIDTH

SUBLANES = 8
TS = 512
SUBT = 256
SEG = SUBT // SUBLANES
SCAN_W = 512
MM_PANEL = 256
VMEM_LIMIT_BYTES = 62 * 1024 * 1024

F32 = jnp.float32
BF16 = jnp.bfloat16


def _dot(a, b):
    return jnp.dot(a, b, preferred_element_type=F32)


def _dot_nt(a, b):
    return lax.dot_general(a, b, (((1,), (1,)), ((), ())), preferred_element_type=F32)


def _dot_tn(a, b):
    return lax.dot_general(a, b, (((0,), (0,)), ((), ())), preferred_element_type=F32)


def _rms(x):
    return x * lax.rsqrt(jnp.mean(x * x, axis=-1, keepdims=True) + NORM_EPS)


def _sigmoid(x):
    return 0.5 * jnp.tanh(0.5 * x) + 0.5


def _cmul_add(ar, ai, br, bi, cr, ci):
    return ar * br - ai * bi + cr, ar * bi + ai * br + ci


def _interleave(*streams):
    totals = [sum(c for c, _ in s) or 1.0 for s in streams]
    pos = [0] * len(streams)
    done = [0.0] * len(streams)
    while True:
        live = [i for i, s in enumerate(streams) if pos[i] < len(s)]
        if not live:
            return
        i = min(live, key=lambda k: done[k] / totals[k])
        cost, thunk = streams[i][pos[i]]
        thunk()
        pos[i] += 1
        done[i] += cost


def _block_kernel(x_ref, p_ref, w_in_ref, w_o_hg_ref, w_glu_ref, w_o_s5_ref, w_out_ref, w_ple_ref,
                  w_ple_gate_ref, w_b_ref, w_c_ref, tri_ref, perm_ref, perm_t_ref, lam_ref, seg_ref,
                  norm_g_ref, lb_ref, hg_norm_g_ref, b_glu_ref, s5_d_ref, ple_norm_g_ref, final_norm_g_ref,
                  out_ref,
                  u_bf, fb, proj, kscr, o_scr, s5p, xs, state, carry, pe_s, qa_s, ka_s, qd_s, kd_s, iv_s):
    ts = x_ref.shape[0]
    n_chunks = ts // HG_CHUNK
    subs = [slice(r, r + SUBT) for r in range(0, ts, SUBT)]
    half_in = S5_WIDTH // 2
    half_st = 2 * (S5_STATES // 2)
    n_re = S5_STATES // 2

    @pl.when(pl.program_id(1) == 0)
    def _():
        state[...] = jnp.zeros_like(state)
        carry[...] = jnp.zeros_like(carry)

    def in_proj(dst, dst_col, src_col, n_cols):
        out = []
        for j in range(0, n_cols, MM_PANEL):
            def thunk(j=j):
                dst[:, dst_col + j:dst_col + j + MM_PANEL] = _dot(
                    u_bf[...], w_in_ref[:, src_col + j:src_col + j + MM_PANEL])
            out.append((MM_PANEL * D_MODEL, thunk))
        return out

    u_bf[...] = (_rms(x_ref[...]) * norm_g_ref[...]).astype(BF16)
    for _, thunk in in_proj(fb, 0, COL_F, HG_WIDTH):
        thunk()

    def gate_math(c):
        rows = slice(c * HG_CHUNK, (c + 1) * HG_CHUNK)
        lb = lb_ref[...]
        sig = _sigmoid(fb[rows, :])
        kscr[rows, :] = (1.0 - lb) * (1.0 - sig)
        logf = jnp.log(lb + (1.0 - lb) * sig)
        logf_hi = logf.astype(BF16)
        qa_s[rows, :] = logf_hi
        ka_s[rows, :] = (logf - logf_hi.astype(F32)).astype(BF16)

    def s5_perm():
        for rs in subs:
            qd_s[rs, 0:S5_WIDTH] = _dot(perm_ref[...], s5p[rs, 0:S5_WIDTH].astype(BF16)).astype(BF16)

    lanes = [(hf, slice(hf * half_st + lc, hf * half_st + lc + SCAN_W),
              slice(hf * half_st + n_re + lc, hf * half_st + n_re + lc + SCAN_W), lc)
             for hf in range(2) for lc in range(0, n_re, SCAN_W)]

    def s5_in(n):
        hf, cre, cim, lc = lanes[n]
        u_s = qd_s[:, hf * half_in:(hf + 1) * half_in]
        xs[:, cre] = _dot(u_s, w_b_ref[hf, :, lc:lc + SCAN_W])
        xs[:, cim] = _dot(u_s, w_b_ref[hf, :, n_re + lc:n_re + lc + SCAN_W])

    _interleave([(1.0, functools.partial(gate_math, c)) for c in range(n_chunks)],
                in_proj(s5p, 0, COL_S5, 2 * S5_WIDTH) + [(ts * S5_WIDTH, s5_perm),
                                                         (2 * SCAN_W * half_in, functools.partial(s5_in, 0))])

    def cumsum(rs, j):
        cols = slice(j, j + MM_PANEL)
        fb[rs, cols] = _dot(tri_ref[...], qa_s[rs, cols]) + _dot(tri_ref[...], ka_s[rs, cols])

    main_mm = ([(2 * MM_PANEL * SUBT, functools.partial(cumsum, rs, j))
                for rs in subs for j in range(0, HG_WIDTH, MM_PANEL)]
               + in_proj(proj, 0, COL_MAIN, MAIN_COLS))

    sub = lax.broadcasted_iota(jnp.int32, (SUBLANES, SCAN_W), 0)
    y_acc = [None, None]

    def scan_items(n, rs):
        _, cre, cim, _ = lanes[n]
        run = [None]

        def step(i, store):
            rows = slice(rs.start + i * SUBLANES, rs.start + (i + 1) * SUBLANES)
            hr, hi = xs[rows, cre], xs[rows, cim]
            if run[0] is not None:
                hr, hi = _cmul_add(lam_ref[:, cre], lam_ref[:, cim], *run[0], hr, hi)
            if store:
                xs[rows, cre] = hr
                xs[rows, cim] = hi
            run[0] = (hr, hi)

        def seg_starts():
            gr, gi = run[0]
            for k in range(3):
                gr, gi = _cmul_add(seg_ref[k, :, cre], seg_ref[k, :, cim],
                                   pltpu.roll(gr, 1 << k, 0), pltpu.roll(gi, 1 << k, 0), gr, gi)
            gr = jnp.where(sub == 0, 0.0, pltpu.roll(gr, 1, 0))
            gi = jnp.where(sub == 0, 0.0, pltpu.roll(gi, 1, 0))
            run[0] = _cmul_add(seg_ref[3, :, cre], seg_ref[3, :, cim], carry[:, cre], carry[:, cim], gr, gi)

        def carry_out():
            hr, hi = run[0]
            carry[:, cre] = jnp.broadcast_to(hr[SUBLANES - 1:SUBLANES, :], (SUBLANES, SCAN_W))
            carry[:, cim] = jnp.broadcast_to(hi[SUBLANES - 1:SUBLANES, :], (SUBLANES, SCAN_W))

        return ([(1.0, functools.partial(step, i, False)) for i in range(SEG)] + [(1.0, seg_starts)]
                + [(1.0, functools.partial(step, i, True)) for i in range(SEG)] + [(0.1, carry_out)])

    def s5_out_part(n):
        hf, cre, cim, lc = lanes[n]
        part = (_dot(xs[:, cre].astype(BF16), w_c_ref[hf, lc:lc + SCAN_W, :])
                + _dot(xs[:, cim].astype(BF16), w_c_ref[hf, n_re + lc:n_re + lc + SCAN_W, :]))
        y_acc[hf] = part if y_acc[hf] is None else y_acc[hf] + part

    share = -(-len(main_mm) // len(lanes))
    for n in range(len(lanes)):
        beside = []
        if n + 1 < len(lanes):
            beside.append((2 * SCAN_W * half_in, functools.partial(s5_in, n + 1)))
        if n > 0:
            beside.append((2 * SCAN_W * half_in, functools.partial(s5_out_part, n - 1)))
        _interleave([item for rs in subs for item in scan_items(n, rs)],
                    beside + main_mm[n * share:(n + 1) * share])
    s5_out_part(len(lanes) - 1)

    decay = [None] * n_chunks

    def chunk_math(c):
        rows = slice(c * HG_CHUNK, (c + 1) * HG_CHUNK)
        b = fb[rows, :]
        b_mid = b[HG_CHUNK // 2 - 1:HG_CHUNK // 2, :]
        b_last = b[HG_CHUNK - 1:HG_CHUNK, :]
        qa = proj[rows, P_Q:P_Q + HG_WIDTH] * jnp.exp(b - b_mid)
        ka = kscr[rows, :] * jnp.exp(b_mid - b)
        qd_s[rows, :] = (qa * jnp.exp(b_mid)).astype(BF16)
        kd_s[rows, :] = (ka * jnp.exp(b_last - b_mid)).astype(BF16)
        qa_s[rows, :] = qa.astype(BF16)
        ka_s[rows, :] = ka.astype(BF16)
        iv_s[rows, :] = proj[rows, P_IV:P_IV + HG_WIDTH].astype(BF16)
        decay[c] = jnp.exp(b_last)

    def s5_out_a():
        y_p = jnp.concatenate(y_acc, axis=1)
        y_hi = y_p.astype(BF16)
        y_lo = (y_p - y_hi.astype(F32)).astype(BF16)
        for rs in subs:
            y_s = (_dot(perm_t_ref[...], y_hi[rs, :]) + _dot(perm_t_ref[...], y_lo[rs, :])
                   + s5_d_ref[...] * s5p[rs, 0:S5_WIDTH])
            xs[rs, 0:S5_WIDTH] = jax.nn.gelu(y_s)

    def s5_out_b():
        glu = _dot(xs[:, 0:S5_WIDTH].astype(BF16), w_glu_ref[...]) + b_glu_ref[...]
        z_s = s5p[:, S5_WIDTH:2 * S5_WIDTH]
        ys = glu[:, 0:S5_WIDTH] * _sigmoid(glu[:, S5_WIDTH:2 * S5_WIDTH]) * (z_s * _sigmoid(z_s))
        xs[:, D_MODEL:2 * D_MODEL] = _dot(ys.astype(BF16), w_o_s5_ref[...])

    def ple():
        pe_s[...] = _dot(p_ref[...].astype(BF16), w_ple_ref[...])

    _interleave([(1.0, functools.partial(chunk_math, c)) for c in range(n_chunks)],
                [(2.0, s5_out_a), (2.0, s5_out_b), (0.5, ple)])

    row = lax.broadcasted_iota(jnp.int32, (SUBT, SUBT), 0)
    col = lax.broadcasted_iota(jnp.int32, (SUBT, SUBT), 1)
    causal = (col <= row) & (col >= (row & -HG_CHUNK))

    for rs in subs:
        for h in range(HG_HEADS):
            sl = slice(h * HG_KEY, (h + 1) * HG_KEY)
            scores = jnp.where(causal, _dot_nt(qa_s[rs, sl], ka_s[rs, sl]), 0.0)
            o_scr[rs, sl] = _dot(scores.astype(BF16), iv_s[rs, sl])
    for c in range(n_chunks):
        rows = slice(c * HG_CHUNK, (c + 1) * HG_CHUNK)
        for h in range(HG_HEADS):
            sl = slice(h * HG_KEY, (h + 1) * HG_KEY)
            st = state[h]
            o = o_scr[rows, sl] + _dot_nt(qd_s[rows, sl], st.astype(BF16))
            state[h] = st * decay[c][:, sl] + _dot_tn(iv_s[rows, sl], kd_s[rows, sl])
            o_scr[rows, sl] = _rms(o)

    g_hg = proj[:, P_G:P_G + HG_WIDTH]
    o_act = o_scr[...] * hg_norm_g_ref[...] * (g_hg * _sigmoid(g_hg))
    y_hg = _dot(o_act.astype(BF16), w_o_hg_ref[...])
    merged = (_sigmoid(proj[:, P_GATE_HG:P_GATE_HG + D_MODEL]) * y_hg
              + _sigmoid(proj[:, P_GATE_S5:P_GATE_S5 + D_MODEL]) * xs[:, D_MODEL:2 * D_MODEL])
    h = x_ref[...] + _dot(merged.astype(BF16), w_out_ref[...])
    gate = _sigmoid(_dot((_rms(h) * ple_norm_g_ref[...]).astype(BF16), w_ple_gate_ref[...]))
    h = h + pe_s[...] * gate
    out_ref[...] = _rms(h) * final_norm_g_ref[...]


def _s5_tables(a_re, a_im, log_dt, b_re, b_im, c_re, c_im):
    g, n, c = S5_GROUPS, S5_STATE, S5_GROUP
    gh, hs, hi_ = g // 2, S5_STATES // 2, S5_WIDTH // 2
    hp = lax.Precision.HIGHEST
    dt = jnp.exp(log_dt.astype(F32))[:, None]
    ar, ai = a_re.astype(F32), a_im.astype(F32)

    m = jnp.asarray([1.0] + [float(SEG * j) for j in range(SUBLANES)], F32)[:, None, None]
    mag, ang = jnp.exp(m * (ar * dt)), m * (ai * dt)
    pr, pi = mag * jnp.cos(ang), mag * jnp.sin(ang)
    lr, li = pr[0], pi[0]
    den = ar * ar + ai * ai
    nr = lr - 1.0
    sr = (nr * ar + li * ai) / den
    si = (li * ar - nr * ai) / den
    br, bi = b_re.astype(F32), b_im.astype(F32)
    bbr = sr[..., None] * br - si[..., None] * bi
    bbi = sr[..., None] * bi + si[..., None] * br

    rows_c, cols_n = jnp.arange(hi_), jnp.arange(hs)
    a_in = jnp.stack([bbr, bbi]).reshape(2, 2, gh, n, c).transpose(1, 0, 2, 4, 3).reshape(2, 2, hi_, n)
    tile_n = (cols_n[None, :] % n == jnp.arange(n)[:, None]).astype(F32)
    t_in = jnp.einsum('hprn,nm->hprm', a_in, tile_n, precision=hp)
    t_in = jnp.where(rows_c[:, None] // c == cols_n[None, :] // n, t_in, 0.0)
    w_b = jnp.concatenate([t_in[:, 0], t_in[:, 1]], axis=-1).astype(BF16)

    a_out = jnp.stack([c_re.astype(F32), -c_im.astype(F32)]).reshape(2, 2, gh, c, n)
    a_out = a_out.transpose(1, 0, 2, 4, 3).reshape(2, 2, hs, c)
    tile_c = (rows_c[None, :] % c == jnp.arange(c)[:, None]).astype(F32)
    t_out = jnp.einsum('hprc,cm->hprm', a_out, tile_c, precision=hp)
    t_out = jnp.where(cols_n[:, None] // n == rows_c[None, :] // c, t_out, 0.0)
    w_c = t_out.reshape(2, 2 * hs, hi_).astype(BF16)

    pr, pi = pr.reshape(-1, S5_STATES), pi.reshape(-1, S5_STATES)
    pw = jnp.concatenate([pr[:, :hs], pi[:, :hs], pr[:, hs:], pi[:, hs:]], axis=1)
    rows = jnp.arange(SUBLANES)[:, None]
    bcast = lambda row: jnp.broadcast_to(row[None, :], (SUBLANES, 2 * S5_STATES))
    lam = bcast(pw[0])
    sp = pw[1:]
    seg = jnp.stack([jnp.where(rows >= k, bcast(sp[k]), 0.0) for k in (1, 2, 4)] + [sp])
    return w_b, w_c, lam, seg


def kernel(x, p, norm_g, w_in, hg_lb, hg_norm_g, w_o_hg, s5_a_re, s5_a_im, s5_log_dt, s5_b_re, s5_b_im,
           s5_c_re, s5_c_im, s5_d, w_glu, b_glu, w_o_s5, w_out, ple_norm_g, w_ple, w_ple_gate, final_norm_g):
    bsz, seq, d = x.shape
    depth = w_in.shape[0]
    assert depth == 1 and d == D_MODEL and seq % TS == 0 and TS % SUBT == 0 and SUBT % HG_CHUNK == 0
    l = 0
    lb = jnp.cumsum(jax.nn.softmax(hg_lb.astype(F32), axis=0), axis=0)[l].reshape(1, HG_WIDTH)
    w_b, w_c, lam, seg = _s5_tables(s5_a_re[l], s5_a_im[l], s5_log_dt[l], s5_b_re[l], s5_b_im[l],
                                        s5_c_re[l], s5_c_im[l])
    r = jnp.arange(SUBT)
    tri = ((r[:, None] >= r[None, :]) & (r[:, None] // HG_CHUNK == r[None, :] // HG_CHUNK)).astype(BF16)
    perm = (r[None, :] == (r[:, None] % SUBLANES) * SEG + r[:, None] // SUBLANES).astype(BF16)

    w_in_l = w_in[l].astype(BF16)
    q_c, f_c, i_c, g_c, s5_c, gate_c = (slice(a, b) for a, b in (
        (0, 1024), (1024, 2048), (2048, 3072), (3072, 4096), (4096, 5120), (5120, 7168)))
    w_in_k = jnp.concatenate([w_in_l[:, c] for c in (f_c, s5_c, q_c, i_c, g_c, gate_c)], axis=1)
    weights = [w_in_k, w_o_hg[l].astype(BF16), w_glu[l].astype(BF16), w_o_s5[l].astype(BF16),
               w_out[l].astype(BF16), w_ple[l].astype(BF16), w_ple_gate[l].astype(BF16), w_b, w_c, tri,
               perm, perm.T, lam, seg]
    rows_ = [norm_g[l].reshape(1, d), lb, hg_norm_g[l].reshape(1, HG_WIDTH), b_glu[l].reshape(1, 2 * S5_WIDTH),
             s5_d[l].reshape(1, S5_WIDTH), ple_norm_g[l].reshape(1, d), final_norm_g.reshape(1, d)]
    rows_ = [a.astype(F32) for a in rows_]

    def resident(a):
        nd = a.ndim
        return pl.BlockSpec(a.shape, lambda b, s, _nd=nd: (0,) * _nd, pipeline_mode=pl.Buffered(1))

    in_specs = ([pl.BlockSpec((None, TS, d), lambda b, s: (b, s, 0)),
                 pl.BlockSpec((None, TS, PLE_DIM), lambda b, s: (b, s, 0))]
                + [resident(a) for a in weights] + [resident(a) for a in rows_])
    scratch = [
        pltpu.VMEM((TS, d), BF16),
        pltpu.VMEM((TS, HG_WIDTH), F32),
        pltpu.VMEM((TS, MAIN_COLS), F32),
        pltpu.VMEM((TS, HG_WIDTH), F32),
        pltpu.VMEM((TS, HG_WIDTH), F32),
        pltpu.VMEM((TS, 2 * S5_WIDTH), F32),
        pltpu.VMEM((TS, 2 * S5_STATES), F32),
        pltpu.VMEM((HG_HEADS, HG_VAL, HG_KEY), F32),
        pltpu.VMEM((SUBLANES, 2 * S5_STATES), F32),
        pltpu.VMEM((TS, D_MODEL), F32),
    ] + [pltpu.VMEM((TS, HG_WIDTH), BF16)] * 5
    return pl.pallas_call(
        _block_kernel,
        out_shape=jax.ShapeDtypeStruct((bsz, seq, d), x.dtype),
        grid=(bsz, seq // TS),
        in_specs=in_specs,
        out_specs=pl.BlockSpec((None, TS, d), lambda b, s: (b, s, 0)),
        scratch_shapes=scratch,
        compiler_params=pltpu.CompilerParams(dimension_semantics=("arbitrary", "arbitrary"),
                                             vmem_limit_bytes=VMEM_LIMIT_BYTES),
        name="hgrn2_s5_block",
    )(x, p[l], *weights, *rows_)
```

```python
import functools

import jax
import jax.numpy as jnp
from jax import lax
from jax.experimental import pallas as pl
from jax.experimental.pallas import tpu as pltpu

D_MODEL = 1024
PLE_DIM = 256
HG_HEADS = 8
HG_KEY = 128
HG_VAL = 128
HG_WIDTH = HG_HEADS * HG_VAL
HG_CHUNK = 64
S5_GROUP = 16
S5_WIDTH = 512
S5_GROUPS = S5_WIDTH // S5_GROUP
S5_STATE = 64
S5_STATES = S5_GROUPS * S5_STATE
NORM_EPS = 1e-6

COL_F = 0
COL_S5 = HG_WIDTH
COL_MAIN = COL_S5 + 2 * S5_WIDTH
P_Q, P_IV, P_G, P_GATE_HG, P_GATE_S5 = (k * HG_WIDTH for k in range(5))
MAIN_COLS = 5 * HG_WIDTH

SUBLANES = 8
TS = 512
SUBT = 256
SEG = SUBT // SUBLANES
SCAN_W = 512
MM_PANEL = 256
VMEM_LIMIT_BYTES = 62 * 1024 * 1024

F32 = jnp.float32
BF16 = jnp.bfloat16


def _dot(a, b):
    return jnp.dot(a, b, preferred_element_type=F32)


def _dot_nt(a, b):
    return lax.dot_general(a, b, (((1,), (1,)), ((), ())), preferred_element_type=F32)


def _dot_tn(a, b):
    return lax.dot_general(a, b, (((0,), (0,)), ((), ())), preferred_element_type=F32)


def _rms(x):
    return x * lax.rsqrt(jnp.mean(x * x, axis=-1, keepdims=True) + NORM_EPS)


def _sigmoid(x):
    return 0.5 * jnp.tanh(0.5 * x) + 0.5


def _cmul_add(ar, ai, br, bi, cr, ci):
    return ar * br - ai * bi + cr, ar * bi + ai * br + ci


def _interleave(*streams):
    totals = [sum(c for c, _ in s) or 1.0 for s in streams]
    pos = [0] * len(streams)
    done = [0.0] * len(streams)
    while True:
        live = [i for i, s in enumerate(streams) if pos[i] < len(s)]
        if not live:
            return
        i = min(live, key=lambda k: done[k] / totals[k])
        cost, thunk = streams[i][pos[i]]
        thunk()
        pos[i] += 1
        done[i] += cost


def _block_kernel(x_ref, p_ref, w_in_ref, w_o_hg_ref, w_glu_ref, w_o_s5_ref, w_out_ref, w_ple_ref,
                  w_ple_gate_ref, w_b_ref, w_c_ref, perm_ref, perm_t_ref, lam_ref, seg_ref,
                  norm_g_ref, lb_ref, hg_norm_g_ref, b_glu_ref, s5_d_ref, ple_norm_g_ref, final_norm_g_ref,
                  out_ref,
                  u_bf, fb, proj, kscr, o_scr, s5p, xs, state, carry, pe_s, qa_s, ka_s, qd_s, kd_s, iv_s):
    ts = x_ref.shape[0]
    n_chunks = ts // HG_CHUNK
    subs = [slice(r, r + SUBT) for r in range(0, ts, SUBT)]
    half_in = S5_WIDTH // 2
    half_st = 2 * (S5_STATES // 2)
    n_re = S5_STATES // 2

    @pl.when(pl.program_id(1) == 0)
    def _():
        state[...] = jnp.zeros_like(state)
        carry[...] = jnp.zeros_like(carry)

    def in_proj(dst, dst_col, src_col, n_cols):
        out = []
        for j in range(0, n_cols, MM_PANEL):
            def thunk(j=j):
                dst[:, dst_col + j:dst_col + j + MM_PANEL] = _dot(
                    u_bf[...], w_in_ref[:, src_col + j:src_col + j + MM_PANEL])
            out.append((MM_PANEL * D_MODEL, thunk))
        return out

    u_bf[...] = (_rms(x_ref[...]) * norm_g_ref[...]).astype(BF16)
    for _, thunk in in_proj(fb, 0, COL_F, HG_WIDTH):
        thunk()

    sub_row = lax.broadcasted_iota(jnp.int32, (SUBLANES, HG_WIDTH), 0)

    def gate_math(c):
        rows = slice(c * HG_CHUNK, (c + 1) * HG_CHUNK)
        lb = lb_ref[...]
        sig = _sigmoid(fb[rows, :])
        kscr[rows, :] = (1.0 - lb) * (1.0 - sig)
        logf = jnp.log(lb + (1.0 - lb) * sig)
        offset = None
        for r in range(0, HG_CHUNK, SUBLANES):
            blk = logf[r:r + SUBLANES, :]
            for d in (1, 2, 4):
                blk = blk + jnp.where(sub_row >= d, pltpu.roll(blk, d, 0), 0.0)
            if offset is not None:
                blk = blk + offset
            offset = jnp.broadcast_to(blk[SUBLANES - 1:SUBLANES, :], (SUBLANES, HG_WIDTH))
            fb[c * HG_CHUNK + r:c * HG_CHUNK + r + SUBLANES, :] = blk

    def s5_perm():
        for rs in subs:
            qd_s[rs, 0:S5_WIDTH] = _dot(perm_ref[...], s5p[rs, 0:S5_WIDTH].astype(BF16)).astype(BF16)

    lanes = [(hf, slice(hf * half_st + lc, hf * half_st + lc + SCAN_W),
              slice(hf * half_st + n_re + lc, hf * half_st + n_re + lc + SCAN_W), lc)
             for hf in range(2) for lc in range(0, n_re, SCAN_W)]

    def s5_in(n):
        hf, cre, cim, lc = lanes[n]
        u_s = qd_s[:, hf * half_in:(hf + 1) * half_in]
        xs[:, cre] = _dot(u_s, w_b_ref[hf, :, lc:lc + SCAN_W])
        xs[:, cim] = _dot(u_s, w_b_ref[hf, :, n_re + lc:n_re + lc + SCAN_W])

    _interleave([(1.0, functools.partial(gate_math, c)) for c in range(n_chunks)],
                in_proj(s5p, 0, COL_S5, 2 * S5_WIDTH) + [(ts * S5_WIDTH, s5_perm),
                                                         (2 * SCAN_W * half_in, functools.partial(s5_in, 0))])

    main_mm = in_proj(proj, 0, COL_MAIN, MAIN_COLS)

    sub = lax.broadcasted_iota(jnp.int32, (SUBLANES, SCAN_W), 0)
    y_acc = [None, None]

    def scan_items(n, rs):
        _, cre, cim, _ = lanes[n]
        run = [None]

        def step(i, store):
            rows = slice(rs.start + i * SUBLANES, rs.start + (i + 1) * SUBLANES)
            hr, hi = xs[rows, cre], xs[rows, cim]
            if run[0] is not None:
                hr, hi = _cmul_add(lam_ref[:, cre], lam_ref[:, cim], *run[0], hr, hi)
            if store:
                xs[rows, cre] = hr
                xs[rows, cim] = hi
            run[0] = (hr, hi)

        def seg_starts():
            gr, gi = run[0]
            for k in range(3):
                gr, gi = _cmul_add(seg_ref[k, :, cre], seg_ref[k, :, cim],
                                   pltpu.roll(gr, 1 << k, 0), pltpu.roll(gi, 1 << k, 0), gr, gi)
            gr = jnp.where(sub == 0, 0.0, pltpu.roll(gr, 1, 0))
            gi = jnp.where(sub == 0, 0.0, pltpu.roll(gi, 1, 0))
            run[0] = _cmul_add(seg_ref[3, :, cre], seg_ref[3, :, cim], carry[:, cre], carry[:, cim], gr, gi)

        def carry_out():
            hr, hi = run[0]
            carry[:, cre] = jnp.broadcast_to(hr[SUBLANES - 1:SUBLANES, :], (SUBLANES, SCAN_W))
            carry[:, cim] = jnp.broadcast_to(hi[SUBLANES - 1:SUBLANES, :], (SUBLANES, SCAN_W))

        return ([(1.0, functools.partial(step, i, False)) for i in range(SEG)] + [(1.0, seg_starts)]
                + [(1.0, functools.partial(step, i, True)) for i in range(SEG)] + [(0.1, carry_out)])

    def s5_out_part(n):
        hf, cre, cim, lc = lanes[n]
        part = (_dot(xs[:, cre].astype(BF16), w_c_ref[hf, lc:lc + SCAN_W, :])
                + _dot(xs[:, cim].astype(BF16), w_c_ref[hf, n_re + lc:n_re + lc + SCAN_W, :]))
        y_acc[hf] = part if y_acc[hf] is None else y_acc[hf] + part

    share = -(-len(main_mm) // len(lanes))
    for n in range(len(lanes)):
        beside = []
        if n + 1 < len(lanes):
            beside.append((2 * SCAN_W * half_in, functools.partial(s5_in, n + 1)))
        if n > 0:
            beside.append((2 * SCAN_W * half_in, functools.partial(s5_out_part, n - 1)))
        _interleave([item for rs in subs for item in scan_items(n, rs)],
                    beside + main_mm[n * share:(n + 1) * share])
    s5_out_part(len(lanes) - 1)

    decay = [None] * n_chunks

    def chunk_math(c):
        rows = slice(c * HG_CHUNK, (c + 1) * HG_CHUNK)
        b = fb[rows, :]
        b_mid = b[HG_CHUNK // 2 - 1:HG_CHUNK // 2, :]
        b_last = b[HG_CHUNK - 1:HG_CHUNK, :]
        qa = proj[rows, P_Q:P_Q + HG_WIDTH] * jnp.exp(b - b_mid)
        ka = kscr[rows, :] * jnp.exp(b_mid - b)
        qd_s[rows, :] = (qa * jnp.exp(b_mid)).astype(BF16)
        kd_s[rows, :] = (ka * jnp.exp(b_last - b_mid)).astype(BF16)
        qa_s[rows, :] = qa.astype(BF16)
        ka_s[rows, :] = ka.astype(BF16)
        iv_s[rows, :] = proj[rows, P_IV:P_IV + HG_WIDTH].astype(BF16)
        decay[c] = jnp.exp(b_last)

    def s5_out_a():
        y_p = jnp.concatenate(y_acc, axis=1)
        y_hi = y_p.astype(BF16)
        y_lo = (y_p - y_hi.astype(F32)).astype(BF16)
        for rs in subs:
            y_s = (_dot(perm_t_ref[...], y_hi[rs, :]) + _dot(perm_t_ref[...], y_lo[rs, :])
                   + s5_d_ref[...] * s5p[rs, 0:S5_WIDTH])
            xs[rs, 0:S5_WIDTH] = jax.nn.gelu(y_s)

    def s5_out_b():
        glu = _dot(xs[:, 0:S5_WIDTH].astype(BF16), w_glu_ref[...]) + b_glu_ref[...]
        z_s = s5p[:, S5_WIDTH:2 * S5_WIDTH]
        ys = glu[:, 0:S5_WIDTH] * _sigmoid(glu[:, S5_WIDTH:2 * S5_WIDTH]) * (z_s * _sigmoid(z_s))
        xs[:, D_MODEL:2 * D_MODEL] = _dot(ys.astype(BF16), w_o_s5_ref[...])

    def ple():
        pe_s[...] = _dot(p_ref[...].astype(BF16), w_ple_ref[...])

    _interleave([(1.0, functools.partial(chunk_math, c)) for c in range(n_chunks)],
                [(2.0, s5_out_a), (2.0, s5_out_b), (0.5, ple)])

    row = lax.broadcasted_iota(jnp.int32, (SUBT, SUBT), 0)
    col = lax.broadcasted_iota(jnp.int32, (SUBT, SUBT), 1)
    causal = (col <= row) & (col >= (row & -HG_CHUNK))

    for rs in subs:
        for h in range(HG_HEADS):
            sl = slice(h * HG_KEY, (h + 1) * HG_KEY)
            scores = jnp.where(causal, _dot_nt(qa_s[rs, sl], ka_s[rs, sl]), 0.0)
            o_scr[rs, sl] = _dot(scores.astype(BF16), iv_s[rs, sl])
    for c in range(n_chunks):
        rows = slice(c * HG_CHUNK, (c + 1) * HG_CHUNK)
        for h in range(HG_HEADS):
            sl = slice(h * HG_KEY, (h + 1) * HG_KEY)
            st = state[h]
            o = o_scr[rows, sl] + _dot_nt(qd_s[rows, sl], st.astype(BF16))
            state[h] = st * decay[c][:, sl] + _dot_tn(iv_s[rows, sl], kd_s[rows, sl])
            o_scr[rows, sl] = _rms(o)

    g_hg = proj[:, P_G:P_G + HG_WIDTH]
    o_act = o_scr[...] * hg_norm_g_ref[...] * (g_hg * _sigmoid(g_hg))
    y_hg = _dot(o_act.astype(BF16), w_o_hg_ref[...])
    merged = (_sigmoid(proj[:, P_GATE_HG:P_GATE_HG + D_MODEL]) * y_hg
              + _sigmoid(proj[:, P_GATE_S5:P_GATE_S5 + D_MODEL]) * xs[:, D_MODEL:2 * D_MODEL])
    h = x_ref[...] + _dot(merged.astype(BF16), w_out_ref[...])
    gate = _sigmoid(_dot((_rms(h) * ple_norm_g_ref[...]).astype(BF16), w_ple_gate_ref[...]))
    h = h + pe_s[...] * gate
    out_ref[...] = _rms(h) * final_norm_g_ref[...]


def _s5_tables(a_re, a_im, log_dt, b_re, b_im, c_re, c_im):
    g, n, c = S5_GROUPS, S5_STATE, S5_GROUP
    gh, hs, hi_ = g // 2, S5_STATES // 2, S5_WIDTH // 2
    hp = lax.Precision.HIGHEST
    dt = jnp.exp(log_dt.astype(F32))[:, None]
    ar, ai = a_re.astype(F32), a_im.astype(F32)

    m = jnp.asarray([1.0] + [float(SEG * j) for j in range(SUBLANES)], F32)[:, None, None]
    mag, ang = jnp.exp(m * (ar * dt)), m * (ai * dt)
    pr, pi = mag * jnp.cos(ang), mag * jnp.sin(ang)
    lr, li = pr[0], pi[0]
    den = ar * ar + ai * ai
    nr = lr - 1.0
    sr = (nr * ar + li * ai) / den
    si = (li * ar - nr * ai) / den
    br, bi = b_re.astype(F32), b_im.astype(F32)
    bbr = sr[..., None] * br - si[..., None] * bi
    bbi = sr[..., None] * bi + si[..., None] * br

    rows_c, cols_n = jnp.arange(hi_), jnp.arange(hs)
    a_in = jnp.stack([bbr, bbi]).reshape(2, 2, gh, n, c).transpose(1, 0, 2, 4, 3).reshape(2, 2, hi_, n)
    tile_n = (cols_n[None, :] % n == jnp.arange(n)[:, None]).astype(F32)
    t_in = jnp.einsum('hprn,nm->hprm', a_in, tile_n, precision=hp)
    t_in = jnp.where(rows_c[:, None] // c == cols_n[None, :] // n, t_in, 0.0)
    w_b = jnp.concatenate([t_in[:, 0], t_in[:, 1]], axis=-1).astype(BF16)

    a_out = jnp.stack([c_re.astype(F32), -c_im.astype(F32)]).reshape(2, 2, gh, c, n)
    a_out = a_out.transpose(1, 0, 2, 4, 3).reshape(2, 2, hs, c)
    tile_c = (rows_c[None, :] % c == jnp.arange(c)[:, None]).astype(F32)
    t_out = jnp.einsum('hprc,cm->hprm', a_out, tile_c, precision=hp)
    t_out = jnp.where(cols_n[:, None] // n == rows_c[None, :] // c, t_out, 0.0)
    w_c = t_out.reshape(2, 2 * hs, hi_).astype(BF16)

    pr, pi = pr.reshape(-1, S5_STATES), pi.reshape(-1, S5_STATES)
    pw = jnp.concatenate([pr[:, :hs], pi[:, :hs], pr[:, hs:], pi[:, hs:]], axis=1)
    rows = jnp.arange(SUBLANES)[:, None]
    bcast = lambda row: jnp.broadcast_to(row[None, :], (SUBLANES, 2 * S5_STATES))
    lam = bcast(pw[0])
    sp = pw[1:]
    seg = jnp.stack([jnp.where(rows >= k, bcast(sp[k]), 0.0) for k in (1, 2, 4)] + [sp])
    return w_b, w_c, lam, seg


def kernel(x, p, norm_g, w_in, hg_lb, hg_norm_g, w_o_hg, s5_a_re, s5_a_im, s5_log_dt, s5_b_re, s5_b_im,
           s5_c_re, s5_c_im, s5_d, w_glu, b_glu, w_o_s5, w_out, ple_norm_g, w_ple, w_ple_gate, final_norm_g):
    bsz, seq, d = x.shape
    depth = w_in.shape[0]
    assert depth == 1 and d == D_MODEL and seq % TS == 0 and TS % SUBT == 0 and SUBT % HG_CHUNK == 0
    l = 0
    lb = jnp.cumsum(jax.nn.softmax(hg_lb.astype(F32), axis=0), axis=0)[l].reshape(1, HG_WIDTH)
    w_b, w_c, lam, seg = _s5_tables(s5_a_re[l], s5_a_im[l], s5_log_dt[l], s5_b_re[l], s5_b_im[l],
                                        s5_c_re[l], s5_c_im[l])
    r = jnp.arange(SUBT)
    perm = (r[None, :] == (r[:, None] % SUBLANES) * SEG + r[:, None] // SUBLANES).astype(BF16)

    w_in_l = w_in[l].astype(BF16)
    q_c, f_c, i_c, g_c, s5_c, gate_c = (slice(a, b) for a, b in (
        (0, 1024), (1024, 2048), (2048, 3072), (3072, 4096), (4096, 5120), (5120, 7168)))
    w_in_k = jnp.concatenate([w_in_l[:, c] for c in (f_c, s5_c, q_c, i_c, g_c, gate_c)], axis=1)
    weights = [w_in_k, w_o_hg[l].astype(BF16), w_glu[l].astype(BF16), w_o_s5[l].astype(BF16),
               w_out[l].astype(BF16), w_ple[l].astype(BF16), w_ple_gate[l].astype(BF16), w_b, w_c,
               perm, perm.T, lam, seg]
    rows_ = [norm_g[l].reshape(1, d), lb, hg_norm_g[l].reshape(1, HG_WIDTH), b_glu[l].reshape(1, 2 * S5_WIDTH),
             s5_d[l].reshape(1, S5_WIDTH), ple_norm_g[l].reshape(1, d), final_norm_g.reshape(1, d)]
    rows_ = [a.astype(F32) for a in rows_]

    def resident(a):
        nd = a.ndim
        return pl.BlockSpec(a.shape, lambda b, s, _nd=nd: (0,) * _nd, pipeline_mode=pl.Buffered(1))

    in_specs = ([pl.BlockSpec((None, TS, d), lambda b, s: (b, s, 0)),
                 pl.BlockSpec((None, TS, PLE_DIM), lambda b, s: (b, s, 0))]
                + [resident(a) for a in weights] + [resident(a) for a in rows_])
    scratch = [
        pltpu.VMEM((TS, d), BF16),
        pltpu.VMEM((TS, HG_WIDTH), F32),
        pltpu.VMEM((TS, MAIN_COLS), F32),
        pltpu.VMEM((TS, HG_WIDTH), F32),
        pltpu.VMEM((TS, HG_WIDTH), F32),
        pltpu.VMEM((TS, 2 * S5_WIDTH), F32),
        pltpu.VMEM((TS, 2 * S5_STATES), F32),
        pltpu.VMEM((HG_HEADS, HG_VAL, HG_KEY), F32),
        pltpu.VMEM((SUBLANES, 2 * S5_STATES), F32),
        pltpu.VMEM((TS, D_MODEL), F32),
    ] + [pltpu.VMEM((TS, HG_WIDTH), BF16)] * 5
    return pl.pallas_call(
        _block_kernel,
        out_shape=jax.ShapeDtypeStruct((bsz, seq, d), x.dtype),
        grid=(bsz, seq // TS),
        in_specs=in_specs,
        out_specs=pl.BlockSpec((None, TS, d), lambda b, s: (b, s, 0)),
        scratch_shapes=scratch,
        compiler_params=pltpu.CompilerParams(dimension_semantics=("arbitrary", "arbitrary"),
                                             vmem_limit_bytes=VMEM_LIMIT_BYTES),
        name="hgrn2_s5_block",
    )(x, p[l], *weights, *rows_)
```

```python
import functools

import jax
import jax.numpy as jnp
from jax import lax
from jax.experimental import pallas as pl
from jax.experimental.pallas import tpu as pltpu

D_MODEL = 1024
PLE_DIM = 256
HG_HEADS = 8
HG_KEY = 128
HG_VAL = 128
HG_WIDTH = HG_HEADS * HG_VAL
HG_CHUNK = 64
S5_GROUP = 16
S5_WIDTH = 512
S5_GROUPS = S5_WIDTH // S5_GROUP
S5_STATE = 64
S5_STATES = S5_GROUPS * S5_STATE
NORM_EPS = 1e-6

COL_F = 0
COL_S5 = HG_WIDTH
COL_MAIN = COL_S5 + 2 * S5_WIDTH
COL_Q, COL_IV = COL_MAIN, COL_MAIN + HG_WIDTH
COL_TAIL = COL_MAIN + 2 * HG_WIDTH
P_G, P_GATE_HG, P_GATE_S5 = (k * HG_WIDTH for k in range(3))
TAIL_COLS = 3 * HG_WIDTH

SUBLANES = 8
TS = 512
SUBT = 256
SEG = SUBT // SUBLANES
SCAN_W = 512
MM_PANEL = 256
VMEM_LIMIT_BYTES = 62 * 1024 * 1024

F32 = jnp.float32
BF16 = jnp.bfloat16


def _dot(a, b):
    return jnp.dot(a, b, preferred_element_type=F32)


def _dot_nt(a, b):
    return lax.dot_general(a, b, (((1,), (1,)), ((), ())), preferred_element_type=F32)


def _dot_tn(a, b):
    return lax.dot_general(a, b, (((0,), (0,)), ((), ())), preferred_element_type=F32)


def _rms(x):
    return x * lax.rsqrt(jnp.mean(x * x, axis=-1, keepdims=True) + NORM_EPS)


def _sigmoid(x):
    return 0.5 * jnp.tanh(0.5 * x) + 0.5


def _cmul_add(ar, ai, br, bi, cr, ci):
    return ar * br - ai * bi + cr, ar * bi + ai * br + ci


def _interleave(*streams):
    totals = [sum(c for c, _ in s) or 1.0 for s in streams]
    pos = [0] * len(streams)
    done = [0.0] * len(streams)
    while True:
        live = [i for i, s in enumerate(streams) if pos[i] < len(s)]
        if not live:
            return
        i = min(live, key=lambda k: done[k] / totals[k])
        cost, thunk = streams[i][pos[i]]
        thunk()
        pos[i] += 1
        done[i] += cost


def _block_kernel(x_ref, p_ref, w_in_ref, w_o_hg_ref, w_glu_ref, w_o_s5_ref, w_out_ref, w_ple_ref,
                  w_ple_gate_ref, w_b_ref, w_c_ref, perm_ref, perm_t_ref, lam_ref, seg_ref,
                  norm_g_ref, lb_ref, hg_norm_g_ref, b_glu_ref, s5_d_ref, ple_norm_g_ref, final_norm_g_ref,
                  out_ref,
                  u_bf, fb, proj, o_scr, s5p, xs, state, carry, pe_s, usp_s, qa_s, ka_s, qd_s, kd_s, iv_s):
    ts = x_ref.shape[0]
    n_chunks = ts // HG_CHUNK
    subs = [slice(r, r + SUBT) for r in range(0, ts, SUBT)]
    half_in = S5_WIDTH // 2
    half_st = 2 * (S5_STATES // 2)
    n_re = S5_STATES // 2

    @pl.when(pl.program_id(1) == 0)
    def _():
        state[...] = jnp.zeros_like(state)
        carry[...] = jnp.zeros_like(carry)

    def in_proj(src_col, n_cols, sink):
        def thunk(j):
            sink(j, _dot(u_bf[...], w_in_ref[:, src_col + j:src_col + j + MM_PANEL]))
        return [(MM_PANEL * D_MODEL, functools.partial(thunk, j)) for j in range(0, n_cols, MM_PANEL)]

    def store_to(dst):
        def sink(j, val):
            dst[:, j:j + MM_PANEL] = val
        return sink

    u_bf[...] = (_rms(x_ref[...]) * norm_g_ref[...]).astype(BF16)
    for _, thunk in in_proj(COL_F, HG_WIDTH, store_to(fb)):
        thunk()

    sub_row = lax.broadcasted_iota(jnp.int32, (SUBLANES, HG_WIDTH), 0)
    decay = [None] * n_chunks

    def gate_math(c):
        rows = slice(c * HG_CHUNK, (c + 1) * HG_CHUNK)
        lb = lb_ref[...]
        sig = _sigmoid(fb[rows, :])
        k_in = (1.0 - lb) * (1.0 - sig)
        logf = jnp.log(lb + (1.0 - lb) * sig)
        offset = None
        for r in range(0, HG_CHUNK, SUBLANES):
            blk = logf[r:r + SUBLANES, :]
            for d in (1, 2, 4):
                blk = blk + jnp.where(sub_row >= d, pltpu.roll(blk, d, 0), 0.0)
            if offset is not None:
                blk = blk + offset
            offset = jnp.broadcast_to(blk[SUBLANES - 1:SUBLANES, :], (SUBLANES, HG_WIDTH))
            fb[c * HG_CHUNK + r:c * HG_CHUNK + r + SUBLANES, :] = blk
        b = fb[rows, :]
        b_mid = b[HG_CHUNK // 2 - 1:HG_CHUNK // 2, :]
        b_last = b[HG_CHUNK - 1:HG_CHUNK, :]
        ka = k_in * jnp.exp(b_mid - b)
        ka_s[rows, :] = ka.astype(BF16)
        kd_s[rows, :] = (ka * jnp.exp(b_last - b_mid)).astype(BF16)
        decay[c] = jnp.exp(b_last)

    def s5_perm():
        for rs in subs:
            usp_s[rs, :] = _dot(perm_ref[...], s5p[rs, 0:S5_WIDTH].astype(BF16)).astype(BF16)

    lanes = [(hf, slice(hf * half_st + lc, hf * half_st + lc + SCAN_W),
              slice(hf * half_st + n_re + lc, hf * half_st + n_re + lc + SCAN_W), lc)
             for hf in range(2) for lc in range(0, n_re, SCAN_W)]

    def s5_in(n):
        hf, cre, cim, lc = lanes[n]
        u_s = usp_s[:, hf * half_in:(hf + 1) * half_in]
        xs[:, cre] = _dot(u_s, w_b_ref[hf, :, lc:lc + SCAN_W])
        xs[:, cim] = _dot(u_s, w_b_ref[hf, :, n_re + lc:n_re + lc + SCAN_W])

    _interleave([(1.0, functools.partial(gate_math, c)) for c in range(n_chunks)],
                in_proj(COL_S5, 2 * S5_WIDTH, store_to(s5p)) + [(ts * S5_WIDTH, s5_perm),
                                                         (2 * SCAN_W * half_in, functools.partial(s5_in, 0))])

    def q_sink(j, q):
        cols = slice(j, j + MM_PANEL)
        for c in range(n_chunks):
            rows = slice(c * HG_CHUNK, (c + 1) * HG_CHUNK)
            b = fb[rows, cols]
            b_mid = b[HG_CHUNK // 2 - 1:HG_CHUNK // 2, :]
            qa = q[rows, :] * jnp.exp(b - b_mid)
            qa_s[rows, cols] = qa.astype(BF16)
            qd_s[rows, cols] = (qa * jnp.exp(b_mid)).astype(BF16)

    def iv_sink(j, iv):
        iv_s[:, j:j + MM_PANEL] = iv.astype(BF16)

    main_mm = (in_proj(COL_Q, HG_WIDTH, q_sink) + in_proj(COL_IV, HG_WIDTH, iv_sink)
               + in_proj(COL_TAIL, TAIL_COLS, store_to(proj)))

    sub = lax.broadcasted_iota(jnp.int32, (SUBLANES, SCAN_W), 0)
    y_acc = [None, None]

    def scan_items(n, rs):
        _, cre, cim, _ = lanes[n]
        run = [None]

        def step(i, store):
            rows = slice(rs.start + i * SUBLANES, rs.start + (i + 1) * SUBLANES)
            hr, hi = xs[rows, cre], xs[rows, cim]
            if run[0] is not None:
                hr, hi = _cmul_add(lam_ref[:, cre], lam_ref[:, cim], *run[0], hr, hi)
            if store:
                xs[rows, cre] = hr
                xs[rows, cim] = hi
            run[0] = (hr, hi)

        def seg_starts():
            gr, gi = run[0]
            for k in range(3):
                gr, gi = _cmul_add(seg_ref[k, :, cre], seg_ref[k, :, cim],
                                   pltpu.roll(gr, 1 << k, 0), pltpu.roll(gi, 1 << k, 0), gr, gi)
            gr = jnp.where(sub == 0, 0.0, pltpu.roll(gr, 1, 0))
            gi = jnp.where(sub == 0, 0.0, pltpu.roll(gi, 1, 0))
            run[0] = _cmul_add(seg_ref[3, :, cre], seg_ref[3, :, cim], carry[:, cre], carry[:, cim], gr, gi)

        def carry_out():
            hr, hi = run[0]
            carry[:, cre] = jnp.broadcast_to(hr[SUBLANES - 1:SUBLANES, :], (SUBLANES, SCAN_W))
            carry[:, cim] = jnp.broadcast_to(hi[SUBLANES - 1:SUBLANES, :], (SUBLANES, SCAN_W))

        return ([(1.0, functools.partial(step, i, False)) for i in range(SEG)] + [(1.0, seg_starts)]
                + [(1.0, functools.partial(step, i, True)) for i in range(SEG)] + [(0.1, carry_out)])

    def s5_out_part(n):
        hf, cre, cim, lc = lanes[n]
        part = (_dot(xs[:, cre].astype(BF16), w_c_ref[hf, lc:lc + SCAN_W, :])
                + _dot(xs[:, cim].astype(BF16), w_c_ref[hf, n_re + lc:n_re + lc + SCAN_W, :]))
        y_acc[hf] = part if y_acc[hf] is None else y_acc[hf] + part

    share = -(-len(main_mm) // len(lanes))
    for n in range(len(lanes)):
        beside = []
        if n + 1 < len(lanes):
            beside.append((2 * SCAN_W * half_in, functools.partial(s5_in, n + 1)))
        if n > 0:
            beside.append((2 * SCAN_W * half_in, functools.partial(s5_out_part, n - 1)))
        _interleave([item for rs in subs for item in scan_items(n, rs)],
                    beside + main_mm[n * share:(n + 1) * share])
    s5_out_part(len(lanes) - 1)

    def s5_out_a():
        y_p = jnp.concatenate(y_acc, axis=1)
        y_hi = y_p.astype(BF16)
        y_lo = (y_p - y_hi.astype(F32)).astype(BF16)
        for rs in subs:
            y_s = (_dot(perm_t_ref[...], y_hi[rs, :]) + _dot(perm_t_ref[...], y_lo[rs, :])
                   + s5_d_ref[...] * s5p[rs, 0:S5_WIDTH])
            xs[rs, 0:S5_WIDTH] = jax.nn.gelu(y_s)

    def s5_out_b():
        glu = _dot(xs[:, 0:S5_WIDTH].astype(BF16), w_glu_ref[...]) + b_glu_ref[...]
        z_s = s5p[:, S5_WIDTH:2 * S5_WIDTH]
        ys = glu[:, 0:S5_WIDTH] * _sigmoid(glu[:, S5_WIDTH:2 * S5_WIDTH]) * (z_s * _sigmoid(z_s))
        xs[:, D_MODEL:2 * D_MODEL] = _dot(ys.astype(BF16), w_o_s5_ref[...])

    def ple():
        pe_s[...] = _dot(p_ref[...].astype(BF16), w_ple_ref[...])

    s5_out_a()
    s5_out_b()
    ple()

    row = lax.broadcasted_iota(jnp.int32, (SUBT, SUBT), 0)
    col = lax.broadcasted_iota(jnp.int32, (SUBT, SUBT), 1)
    causal = (col <= row) & (col >= (row & -HG_CHUNK))

    for rs in subs:
        for h in range(HG_HEADS):
            sl = slice(h * HG_KEY, (h + 1) * HG_KEY)
            scores = jnp.where(causal, _dot_nt(qa_s[rs, sl], ka_s[rs, sl]), 0.0)
            o_scr[rs, sl] = _dot(scores.astype(BF16), iv_s[rs, sl])
    for c in range(n_chunks):
        rows = slice(c * HG_CHUNK, (c + 1) * HG_CHUNK)
        for h in range(HG_HEADS):
            sl = slice(h * HG_KEY, (h + 1) * HG_KEY)
            st = state[h]
            o = o_scr[rows, sl] + _dot_nt(qd_s[rows, sl], st.astype(BF16))
            state[h] = st * decay[c][:, sl] + _dot_tn(iv_s[rows, sl], kd_s[rows, sl])
            o_scr[rows, sl] = _rms(o)

    g_hg = proj[:, P_G:P_G + HG_WIDTH]
    o_act = o_scr[...] * hg_norm_g_ref[...] * (g_hg * _sigmoid(g_hg))
    y_hg = _dot(o_act.astype(BF16), w_o_hg_ref[...])
    merged = (_sigmoid(proj[:, P_GATE_HG:P_GATE_HG + D_MODEL]) * y_hg
              + _sigmoid(proj[:, P_GATE_S5:P_GATE_S5 + D_MODEL]) * xs[:, D_MODEL:2 * D_MODEL])
    h = x_ref[...] + _dot(merged.astype(BF16), w_out_ref[...])
    gate = _sigmoid(_dot((_rms(h) * ple_norm_g_ref[...]).astype(BF16), w_ple_gate_ref[...]))
    h = h + pe_s[...] * gate
    out_ref[...] = _rms(h) * final_norm_g_ref[...]


def _s5_tables(a_re, a_im, log_dt, b_re, b_im, c_re, c_im):
    g, n, c = S5_GROUPS, S5_STATE, S5_GROUP
    gh, hs, hi_ = g // 2, S5_STATES // 2, S5_WIDTH // 2
    hp = lax.Precision.HIGHEST
    dt = jnp.exp(log_dt.astype(F32))[:, None]
    ar, ai = a_re.astype(F32), a_im.astype(F32)

    m = jnp.asarray([1.0] + [float(SEG * j) for j in range(SUBLANES)], F32)[:, None, None]
    mag, ang = jnp.exp(m * (ar * dt)), m * (ai * dt)
    pr, pi = mag * jnp.cos(ang), mag * jnp.sin(ang)
    lr, li = pr[0], pi[0]
    den = ar * ar + ai * ai
    nr = lr - 1.0
    sr = (nr * ar + li * ai) / den
    si = (li * ar - nr * ai) / den
    br, bi = b_re.astype(F32), b_im.astype(F32)
    bbr = sr[..., None] * br - si[..., None] * bi
    bbi = sr[..., None] * bi + si[..., None] * br

    rows_c, cols_n = jnp.arange(hi_), jnp.arange(hs)
    a_in = jnp.stack([bbr, bbi]).reshape(2, 2, gh, n, c).transpose(1, 0, 2, 4, 3).reshape(2, 2, hi_, n)
    tile_n = (cols_n[None, :] % n == jnp.arange(n)[:, None]).astype(F32)
    t_in = jnp.einsum('hprn,nm->hprm', a_in, tile_n, precision=hp)
    t_in = jnp.where(rows_c[:, None] // c == cols_n[None, :] // n, t_in, 0.0)
    w_b = jnp.concatenate([t_in[:, 0], t_in[:, 1]], axis=-1).astype(BF16)

    a_out = jnp.stack([c_re.astype(F32), -c_im.astype(F32)]).reshape(2, 2, gh, c, n)
    a_out = a_out.transpose(1, 0, 2, 4, 3).reshape(2, 2, hs, c)
    tile_c = (rows_c[None, :] % c == jnp.arange(c)[:, None]).astype(F32)
    t_out = jnp.einsum('hprc,cm->hprm', a_out, tile_c, precision=hp)
    t_out = jnp.where(cols_n[:, None] // n == rows_c[None, :] // c, t_out, 0.0)
    w_c = t_out.reshape(2, 2 * hs, hi_).astype(BF16)

    pr, pi = pr.reshape(-1, S5_STATES), pi.reshape(-1, S5_STATES)
    pw = jnp.concatenate([pr[:, :hs], pi[:, :hs], pr[:, hs:], pi[:, hs:]], axis=1)
    rows = jnp.arange(SUBLANES)[:, None]
    bcast = lambda row: jnp.broadcast_to(row[None, :], (SUBLANES, 2 * S5_STATES))
    lam = bcast(pw[0])
    sp = pw[1:]
    seg = jnp.stack([jnp.where(rows >= k, bcast(sp[k]), 0.0) for k in (1, 2, 4)] + [sp])
    return w_b, w_c, lam, seg


def kernel(x, p, norm_g, w_in, hg_lb, hg_norm_g, w_o_hg, s5_a_re, s5_a_im, s5_log_dt, s5_b_re, s5_b_im,
           s5_c_re, s5_c_im, s5_d, w_glu, b_glu, w_o_s5, w_out, ple_norm_g, w_ple, w_ple_gate, final_norm_g):
    bsz, seq, d = x.shape
    depth = w_in.shape[0]
    assert depth == 1 and d == D_MODEL and seq % TS == 0 and TS % SUBT == 0 and SUBT % HG_CHUNK == 0
    l = 0
    lb = jnp.cumsum(jax.nn.softmax(hg_lb.astype(F32), axis=0), axis=0)[l].reshape(1, HG_WIDTH)
    w_b, w_c, lam, seg = _s5_tables(s5_a_re[l], s5_a_im[l], s5_log_dt[l], s5_b_re[l], s5_b_im[l],
                                        s5_c_re[l], s5_c_im[l])
    r = jnp.arange(SUBT)
    perm = (r[None, :] == (r[:, None] % SUBLANES) * SEG + r[:, None] // SUBLANES).astype(BF16)

    w_in_l = w_in[l].astype(BF16)
    q_c, f_c, i_c, g_c, s5_c, gate_c = (slice(a, b) for a, b in (
        (0, 1024), (1024, 2048), (2048, 3072), (3072, 4096), (4096, 5120), (5120, 7168)))
    w_in_k = jnp.concatenate([w_in_l[:, c] for c in (f_c, s5_c, q_c, i_c, g_c, gate_c)], axis=1)
    weights = [w_in_k, w_o_hg[l].astype(BF16), w_glu[l].astype(BF16), w_o_s5[l].astype(BF16),
               w_out[l].astype(BF16), w_ple[l].astype(BF16), w_ple_gate[l].astype(BF16), w_b, w_c,
               perm, perm.T, lam, seg]
    rows_ = [norm_g[l].reshape(1, d), lb, hg_norm_g[l].reshape(1, HG_WIDTH), b_glu[l].reshape(1, 2 * S5_WIDTH),
             s5_d[l].reshape(1, S5_WIDTH), ple_norm_g[l].reshape(1, d), final_norm_g.reshape(1, d)]
    rows_ = [a.astype(F32) for a in rows_]

    def resident(a):
        nd = a.ndim
        return pl.BlockSpec(a.shape, lambda b, s, _nd=nd: (0,) * _nd, pipeline_mode=pl.Buffered(1))

    in_specs = ([pl.BlockSpec((None, TS, d), lambda b, s: (b, s, 0)),
                 pl.BlockSpec((None, TS, PLE_DIM), lambda b, s: (b, s, 0))]
                + [resident(a) for a in weights] + [resident(a) for a in rows_])
    scratch = [
        pltpu.VMEM((TS, d), BF16),
        pltpu.VMEM((TS, HG_WIDTH), F32),
        pltpu.VMEM((TS, TAIL_COLS), F32),
        pltpu.VMEM((TS, HG_WIDTH), F32),
        pltpu.VMEM((TS, 2 * S5_WIDTH), F32),
        pltpu.VMEM((TS, 2 * S5_STATES), F32),
        pltpu.VMEM((HG_HEADS, HG_VAL, HG_KEY), F32),
        pltpu.VMEM((SUBLANES, 2 * S5_STATES), F32),
        pltpu.VMEM((TS, D_MODEL), F32),
        pltpu.VMEM((TS, S5_WIDTH), BF16),
    ] + [pltpu.VMEM((TS, HG_WIDTH), BF16)] * 5
    return pl.pallas_call(
        _block_kernel,
        out_shape=jax.ShapeDtypeStruct((bsz, seq, d), x.dtype),
        grid=(bsz, seq // TS),
        in_specs=in_specs,
        out_specs=pl.BlockSpec((None, TS, d), lambda b, s: (b, s, 0)),
        scratch_shapes=scratch,
        compiler_params=pltpu.CompilerParams(dimension_semantics=("arbitrary", "arbitrary"),
                                             vmem_limit_bytes=VMEM_LIMIT_BYTES),
        name="hgrn2_s5_block",
    )(x, p[l], *weights, *rows_)
```

```python
import functools

import jax
import jax.numpy as jnp
from jax import lax
from jax.experimental import pallas as pl
from jax.experimental.pallas import tpu as pltpu

D_MODEL = 1024
PLE_DIM = 256
HG_HEADS = 8
HG_KEY = 128
HG_VAL = 128
HG_WIDTH = HG_HEADS * HG_VAL
HG_CHUNK = 64
S5_GROUP = 16
S5_WIDTH = 512
S5_GROUPS = S5_WIDTH // S5_GROUP
S5_STATE = 64
S5_STATES = S5_GROUPS * S5_STATE
NORM_EPS = 1e-6

COL_F = 0
COL_S5 = HG_WIDTH
COL_MAIN = COL_S5 + 2 * S5_WIDTH
COL_Q, COL_IV = COL_MAIN, COL_MAIN + HG_WIDTH
COL_TAIL = COL_MAIN + 2 * HG_WIDTH
P_G, P_GATE_HG, P_GATE_S5 = (k * HG_WIDTH for k in range(3))
TAIL_COLS = 3 * HG_WIDTH

SUBLANES = 8
TS = 512
SUBT = 256
SEG = SUBT // SUBLANES
SCAN_W = 256
MM_PANEL = 256
VMEM_LIMIT_BYTES = 62 * 1024 * 1024

F32 = jnp.float32
BF16 = jnp.bfloat16


def _dot(a, b):
    return jnp.dot(a, b, preferred_element_type=F32)


def _dot_nt(a, b):
    return lax.dot_general(a, b, (((1,), (1,)), ((), ())), preferred_element_type=F32)


def _dot_tn(a, b):
    return lax.dot_general(a, b, (((0,), (0,)), ((), ())), preferred_element_type=F32)


def _rms(x):
    return x * lax.rsqrt(jnp.mean(x * x, axis=-1, keepdims=True) + NORM_EPS)


def _sigmoid(x):
    return 0.5 * jnp.tanh(0.5 * x) + 0.5


def _cmul_add(ar, ai, br, bi, cr, ci):
    return ar * br - ai * bi + cr, ar * bi + ai * br + ci


def _interleave(*streams):
    totals = [sum(c for c, _ in s) or 1.0 for s in streams]
    pos = [0] * len(streams)
    done = [0.0] * len(streams)
    while True:
        live = [i for i, s in enumerate(streams) if pos[i] < len(s)]
        if not live:
            return
        i = min(live, key=lambda k: done[k] / totals[k])
        cost, thunk = streams[i][pos[i]]
        thunk()
        pos[i] += 1
        done[i] += cost


def _block_kernel(x_ref, p_ref, w_in_ref, w_o_hg_ref, w_glu_ref, w_o_s5_ref, w_out_ref, w_ple_ref,
                  w_ple_gate_ref, w_b_ref, w_c_ref, perm_ref, perm_t_ref, lam_ref, seg_ref,
                  norm_g_ref, lb_ref, hg_norm_g_ref, b_glu_ref, s5_d_ref, ple_norm_g_ref, final_norm_g_ref,
                  out_ref,
                  u_bf, fb, proj, o_scr, s5p, xs, state, carry, pe_s, usp_s, qa_s, ka_s, qd_s, kd_s, iv_s):
    ts = x_ref.shape[0]
    n_chunks = ts // HG_CHUNK
    subs = [slice(r, r + SUBT) for r in range(0, ts, SUBT)]
    half_in = S5_WIDTH // 2
    half_st = 2 * (S5_STATES // 2)
    n_re = S5_STATES // 2

    @pl.when(pl.program_id(1) == 0)
    def _():
        state[...] = jnp.zeros_like(state)
        carry[...] = jnp.zeros_like(carry)

    def in_proj(src_col, n_cols, sink):
        def thunk(j):
            sink(j, _dot(u_bf[...], w_in_ref[:, src_col + j:src_col + j + MM_PANEL]))
        return [(MM_PANEL * D_MODEL, functools.partial(thunk, j)) for j in range(0, n_cols, MM_PANEL)]

    def store_to(dst):
        def sink(j, val):
            dst[:, j:j + MM_PANEL] = val
        return sink

    u_bf[...] = (_rms(x_ref[...]) * norm_g_ref[...]).astype(BF16)
    for _, thunk in in_proj(COL_F, HG_WIDTH, store_to(fb)):
        thunk()

    sub_row = lax.broadcasted_iota(jnp.int32, (SUBLANES, HG_WIDTH), 0)
    decay = [None] * n_chunks

    def gate_math(c):
        rows = slice(c * HG_CHUNK, (c + 1) * HG_CHUNK)
        lb = lb_ref[...]
        sig = _sigmoid(fb[rows, :])
        k_in = (1.0 - lb) * (1.0 - sig)
        logf = jnp.log(lb + (1.0 - lb) * sig)
        offset = None
        for r in range(0, HG_CHUNK, SUBLANES):
            blk = logf[r:r + SUBLANES, :]
            for d in (1, 2, 4):
                blk = blk + jnp.where(sub_row >= d, pltpu.roll(blk, d, 0), 0.0)
            if offset is not None:
                blk = blk + offset
            offset = jnp.broadcast_to(blk[SUBLANES - 1:SUBLANES, :], (SUBLANES, HG_WIDTH))
            fb[c * HG_CHUNK + r:c * HG_CHUNK + r + SUBLANES, :] = blk
        b = fb[rows, :]
        b_mid = b[HG_CHUNK // 2 - 1:HG_CHUNK // 2, :]
        b_last = b[HG_CHUNK - 1:HG_CHUNK, :]
        ka = k_in * jnp.exp(b_mid - b)
        ka_s[rows, :] = ka.astype(BF16)
        kd_s[rows, :] = (ka * jnp.exp(b_last - b_mid)).astype(BF16)
        decay[c] = jnp.exp(b_last)

    def s5_perm():
        for rs in subs:
            usp_s[rs, :] = _dot(perm_ref[...], s5p[rs, 0:S5_WIDTH].astype(BF16)).astype(BF16)

    lanes = [(hf, slice(hf * half_st + lc, hf * half_st + lc + SCAN_W),
              slice(hf * half_st + n_re + lc, hf * half_st + n_re + lc + SCAN_W), lc)
             for hf in range(2) for lc in range(0, n_re, SCAN_W)]

    def s5_in(n):
        hf, cre, cim, lc = lanes[n]
        u_s = usp_s[:, hf * half_in:(hf + 1) * half_in]
        xs[:, cre] = _dot(u_s, w_b_ref[hf, :, lc:lc + SCAN_W])
        xs[:, cim] = _dot(u_s, w_b_ref[hf, :, n_re + lc:n_re + lc + SCAN_W])

    _interleave([(1.0, functools.partial(gate_math, c)) for c in range(n_chunks)],
                in_proj(COL_S5, 2 * S5_WIDTH, store_to(s5p)) + [(ts * S5_WIDTH, s5_perm),
                                                         (2 * SCAN_W * half_in, functools.partial(s5_in, 0))])

    def q_sink(j, q):
        cols = slice(j, j + MM_PANEL)
        for c in range(n_chunks):
            rows = slice(c * HG_CHUNK, (c + 1) * HG_CHUNK)
            b = fb[rows, cols]
            b_mid = b[HG_CHUNK // 2 - 1:HG_CHUNK // 2, :]
            qa = q[rows, :] * jnp.exp(b - b_mid)
            qa_s[rows, cols] = qa.astype(BF16)
            qd_s[rows, cols] = (qa * jnp.exp(b_mid)).astype(BF16)

    def iv_sink(j, iv):
        iv_s[:, j:j + MM_PANEL] = iv.astype(BF16)

    main_mm = (in_proj(COL_Q, HG_WIDTH, q_sink) + in_proj(COL_IV, HG_WIDTH, iv_sink)
               + in_proj(COL_TAIL, TAIL_COLS, store_to(proj)))

    sub = lax.broadcasted_iota(jnp.int32, (SUBLANES, SCAN_W), 0)
    y_acc = [None, None]

    def scan_items(n, rs):
        _, cre, cim, _ = lanes[n]
        run = [None]

        def step(i, store):
            rows = slice(rs.start + i * SUBLANES, rs.start + (i + 1) * SUBLANES)
            hr, hi = xs[rows, cre], xs[rows, cim]
            if run[0] is not None:
                hr, hi = _cmul_add(lam_ref[:, cre], lam_ref[:, cim], *run[0], hr, hi)
            if store:
                xs[rows, cre] = hr
                xs[rows, cim] = hi
            run[0] = (hr, hi)

        def seg_starts():
            gr, gi = run[0]
            for k in range(3):
                gr, gi = _cmul_add(seg_ref[k, :, cre], seg_ref[k, :, cim],
                                   pltpu.roll(gr, 1 << k, 0), pltpu.roll(gi, 1 << k, 0), gr, gi)
            gr = jnp.where(sub == 0, 0.0, pltpu.roll(gr, 1, 0))
            gi = jnp.where(sub == 0, 0.0, pltpu.roll(gi, 1, 0))
            run[0] = _cmul_add(seg_ref[3, :, cre], seg_ref[3, :, cim], carry[:, cre], carry[:, cim], gr, gi)

        def carry_out():
            hr, hi = run[0]
            carry[:, cre] = jnp.broadcast_to(hr[SUBLANES - 1:SUBLANES, :], (SUBLANES, SCAN_W))
            carry[:, cim] = jnp.broadcast_to(hi[SUBLANES - 1:SUBLANES, :], (SUBLANES, SCAN_W))

        return ([(1.0, functools.partial(step, i, False)) for i in range(SEG)] + [(1.0, seg_starts)]
                + [(1.0, functools.partial(step, i, True)) for i in range(SEG)] + [(0.1, carry_out)])

    def s5_out_part(n):
        hf, cre, cim, lc = lanes[n]
        part = (_dot(xs[:, cre].astype(BF16), w_c_ref[hf, lc:lc + SCAN_W, :])
                + _dot(xs[:, cim].astype(BF16), w_c_ref[hf, n_re + lc:n_re + lc + SCAN_W, :]))
        y_acc[hf] = part if y_acc[hf] is None else y_acc[hf] + part

    share = -(-len(main_mm) // len(lanes))
    for n in range(len(lanes)):
        beside = []
        if n + 1 < len(lanes):
            beside.append((2 * SCAN_W * half_in, functools.partial(s5_in, n + 1)))
        if n > 0:
            beside.append((2 * SCAN_W * half_in, functools.partial(s5_out_part, n - 1)))
        _interleave([item for rs in subs for item in scan_items(n, rs)],
                    beside + main_mm[n * share:(n + 1) * share])
    s5_out_part(len(lanes) - 1)

    def s5_out_a():
        y_p = jnp.concatenate(y_acc, axis=1)
        y_hi = y_p.astype(BF16)
        y_lo = (y_p - y_hi.astype(F32)).astype(BF16)
        for rs in subs:
            y_s = (_dot(perm_t_ref[...], y_hi[rs, :]) + _dot(perm_t_ref[...], y_lo[rs, :])
                   + s5_d_ref[...] * s5p[rs, 0:S5_WIDTH])
            xs[rs, 0:S5_WIDTH] = jax.nn.gelu(y_s)

    def s5_out_b():
        glu = _dot(xs[:, 0:S5_WIDTH].astype(BF16), w_glu_ref[...]) + b_glu_ref[...]
        z_s = s5p[:, S5_WIDTH:2 * S5_WIDTH]
        ys = glu[:, 0:S5_WIDTH] * _sigmoid(glu[:, S5_WIDTH:2 * S5_WIDTH]) * (z_s * _sigmoid(z_s))
        xs[:, D_MODEL:2 * D_MODEL] = _dot(ys.astype(BF16), w_o_s5_ref[...])

    def ple():
        pe_s[...] = _dot(p_ref[...].astype(BF16), w_ple_ref[...])

    s5_out_a()
    s5_out_b()
    ple()

    row = lax.broadcasted_iota(jnp.int32, (SUBT, SUBT), 0)
    col = lax.broadcasted_iota(jnp.int32, (SUBT, SUBT), 1)
    causal = (col <= row) & (col >= (row & -HG_CHUNK))

    for rs in subs:
        for h in range(HG_HEADS):
            sl = slice(h * HG_KEY, (h + 1) * HG_KEY)
            scores = jnp.where(causal, _dot_nt(qa_s[rs, sl], ka_s[rs, sl]), 0.0)
            o_scr[rs, sl] = _dot(scores.astype(BF16), iv_s[rs, sl])
    for c in range(n_chunks):
        rows = slice(c * HG_CHUNK, (c + 1) * HG_CHUNK)
        for h in range(HG_HEADS):
            sl = slice(h * HG_KEY, (h + 1) * HG_KEY)
            st = state[h]
            o = o_scr[rows, sl] + _dot_nt(qd_s[rows, sl], st.astype(BF16))
            state[h] = st * decay[c][:, sl] + _dot_tn(iv_s[rows, sl], kd_s[rows, sl])
            o_scr[rows, sl] = _rms(o)

    g_hg = proj[:, P_G:P_G + HG_WIDTH]
    o_act = o_scr[...] * hg_norm_g_ref[...] * (g_hg * _sigmoid(g_hg))
    y_hg = _dot(o_act.astype(BF16), w_o_hg_ref[...])
    merged = (_sigmoid(proj[:, P_GATE_HG:P_GATE_HG + D_MODEL]) * y_hg
              + _sigmoid(proj[:, P_GATE_S5:P_GATE_S5 + D_MODEL]) * xs[:, D_MODEL:2 * D_MODEL])
    h = x_ref[...] + _dot(merged.astype(BF16), w_out_ref[...])
    gate = _sigmoid(_dot((_rms(h) * ple_norm_g_ref[...]).astype(BF16), w_ple_gate_ref[...]))
    h = h + pe_s[...] * gate
    out_ref[...] = _rms(h) * final_norm_g_ref[...]


def _s5_tables(a_re, a_im, log_dt, b_re, b_im, c_re, c_im):
    g, n, c = S5_GROUPS, S5_STATE, S5_GROUP
    gh, hs, hi_ = g // 2, S5_STATES // 2, S5_WIDTH // 2
    hp = lax.Precision.HIGHEST
    dt = jnp.exp(log_dt.astype(F32))[:, None]
    ar, ai = a_re.astype(F32), a_im.astype(F32)

    m = jnp.asarray([1.0] + [float(SEG * j) for j in range(SUBLANES)], F32)[:, None, None]
    mag, ang = jnp.exp(m * (ar * dt)), m * (ai * dt)
    pr, pi = mag * jnp.cos(ang), mag * jnp.sin(ang)
    lr, li = pr[0], pi[0]
    den = ar * ar + ai * ai
    nr = lr - 1.0
    sr = (nr * ar + li * ai) / den
    si = (li * ar - nr * ai) / den
    br, bi = b_re.astype(F32), b_im.astype(F32)
    bbr = sr[..., None] * br - si[..., None] * bi
    bbi = sr[..., None] * bi + si[..., None] * br

    rows_c, cols_n = jnp.arange(hi_), jnp.arange(hs)
    a_in = jnp.stack([bbr, bbi]).reshape(2, 2, gh, n, c).transpose(1, 0, 2, 4, 3).reshape(2, 2, hi_, n)
    tile_n = (cols_n[None, :] % n == jnp.arange(n)[:, None]).astype(F32)
    t_in = jnp.einsum('hprn,nm->hprm', a_in, tile_n, precision=hp)
    t_in = jnp.where(rows_c[:, None] // c == cols_n[None, :] // n, t_in, 0.0)
    w_b = jnp.concatenate([t_in[:, 0], t_in[:, 1]], axis=-1).astype(BF16)

    a_out = jnp.stack([c_re.astype(F32), -c_im.astype(F32)]).reshape(2, 2, gh, c, n)
    a_out = a_out.transpose(1, 0, 2, 4, 3).reshape(2, 2, hs, c)
    tile_c = (rows_c[None, :] % c == jnp.arange(c)[:, None]).astype(F32)
    t_out = jnp.einsum('hprc,cm->hprm', a_out, tile_c, precision=hp)
    t_out = jnp.where(cols_n[:, None] // n == rows_c[None, :] // c, t_out, 0.0)
    w_c = t_out.reshape(2, 2 * hs, hi_).astype(BF16)

    pr, pi = pr.reshape(-1, S5_STATES), pi.reshape(-1, S5_STATES)
    pw = jnp.concatenate([pr[:, :hs], pi[:, :hs], pr[:, hs:], pi[:, hs:]], axis=1)
    rows = jnp.arange(SUBLANES)[:, None]
    bcast = lambda row: jnp.broadcast_to(row[None, :], (SUBLANES, 2 * S5_STATES))
    lam = bcast(pw[0])
    sp = pw[1:]
    seg = jnp.stack([jnp.where(rows >= k, bcast(sp[k]), 0.0) for k in (1, 2, 4)] + [sp])
    return w_b, w_c, lam, seg


def kernel(x, p, norm_g, w_in, hg_lb, hg_norm_g, w_o_hg, s5_a_re, s5_a_im, s5_log_dt, s5_b_re, s5_b_im,
           s5_c_re, s5_c_im, s5_d, w_glu, b_glu, w_o_s5, w_out, ple_norm_g, w_ple, w_ple_gate, final_norm_g):
    bsz, seq, d = x.shape
    depth = w_in.shape[0]
    assert depth == 1 and d == D_MODEL and seq % TS == 0 and TS % SUBT == 0 and SUBT % HG_CHUNK == 0
    l = 0
    lb = jnp.cumsum(jax.nn.softmax(hg_lb.astype(F32), axis=0), axis=0)[l].reshape(1, HG_WIDTH)
    w_b, w_c, lam, seg = _s5_tables(s5_a_re[l], s5_a_im[l], s5_log_dt[l], s5_b_re[l], s5_b_im[l],
                                        s5_c_re[l], s5_c_im[l])
    r = jnp.arange(SUBT)
    perm = (r[None, :] == (r[:, None] % SUBLANES) * SEG + r[:, None] // SUBLANES).astype(BF16)

    w_in_l = w_in[l].astype(BF16)
    q_c, f_c, i_c, g_c, s5_c, gate_c = (slice(a, b) for a, b in (
        (0, 1024), (1024, 2048), (2048, 3072), (3072, 4096), (4096, 5120), (5120, 7168)))
    w_in_k = jnp.concatenate([w_in_l[:, c] for c in (f_c, s5_c, q_c, i_c, g_c, gate_c)], axis=1)
    weights = [w_in_k, w_o_hg[l].astype(BF16), w_glu[l].astype(BF16), w_o_s5[l].astype(BF16),
               w_out[l].astype(BF16), w_ple[l].astype(BF16), w_ple_gate[l].astype(BF16), w_b, w_c,
               perm, perm.T, lam, seg]
    rows_ = [norm_g[l].reshape(1, d), lb, hg_norm_g[l].reshape(1, HG_WIDTH), b_glu[l].reshape(1, 2 * S5_WIDTH),
             s5_d[l].reshape(1, S5_WIDTH), ple_norm_g[l].reshape(1, d), final_norm_g.reshape(1, d)]
    rows_ = [a.astype(F32) for a in rows_]

    def resident(a):
        nd = a.ndim
        return pl.BlockSpec(a.shape, lambda b, s, _nd=nd: (0,) * _nd, pipeline_mode=pl.Buffered(1))

    in_specs = ([pl.BlockSpec((None, TS, d), lambda b, s: (b, s, 0)),
                 pl.BlockSpec((None, TS, PLE_DIM), lambda b, s: (b, s, 0))]
                + [resident(a) for a in weights] + [resident(a) for a in rows_])
    scratch = [
        pltpu.VMEM((TS, d), BF16),
        pltpu.VMEM((TS, HG_WIDTH), F32),
        pltpu.VMEM((TS, TAIL_COLS), F32),
        pltpu.VMEM((TS, HG_WIDTH), F32),
        pltpu.VMEM((TS, 2 * S5_WIDTH), F32),
        pltpu.VMEM((TS, 2 * S5_STATES), F32),
        pltpu.VMEM((HG_HEADS, HG_VAL, HG_KEY), F32),
        pltpu.VMEM((SUBLANES, 2 * S5_STATES), F32),
        pltpu.VMEM((TS, D_MODEL), F32),
        pltpu.VMEM((TS, S5_WIDTH), BF16),
    ] + [pltpu.VMEM((TS, HG_WIDTH), BF16)] * 5
    return pl.pallas_call(
        _block_kernel,
        out_shape=jax.ShapeDtypeStruct((bsz, seq, d), x.dtype),
        grid=(bsz, seq // TS),
        in_specs=in_specs,
        out_specs=pl.BlockSpec((None, TS, d), lambda b, s: (b, s, 0)),
        scratch_shapes=scratch,
        compiler_params=pltpu.CompilerParams(dimension_semantics=("arbitrary", "arbitrary"),
                                             vmem_limit_bytes=VMEM_LIMIT_BYTES),
        name="hgrn2_s5_block",
    )(x, p[l], *weights, *rows_)
```

```python
import functools

import jax
import jax.numpy as jnp
from jax import lax
from jax.experimental import pallas as pl
from jax.experimental.pallas import tpu as pltpu

D_MODEL = 1024
PLE_DIM = 256
HG_HEADS = 8
HG_KEY = 128
HG_VAL = 128
HG_WIDTH = HG_HEADS * HG_VAL
HG_CHUNK = 64
S5_GROUP = 16
S5_WIDTH = 512
S5_GROUPS = S5_WIDTH // S5_GROUP
S5_STATE = 64
S5_STATES = S5_GROUPS * S5_STATE
NORM_EPS = 1e-6

COL_F = 0
COL_S5 = HG_WIDTH
COL_MAIN = COL_S5 + 2 * S5_WIDTH
COL_Q, COL_IV = COL_MAIN, COL_MAIN + HG_WIDTH
COL_TAIL = COL_MAIN + 2 * HG_WIDTH
P_G, P_GATE_HG, P_GATE_S5 = (k * HG_WIDTH for k in range(3))
TAIL_COLS = 3 * HG_WIDTH

SUBLANES = 8
TS = 512
SUBT = 256
SEG = SUBT // SUBLANES
SCAN_W = 256
MM_PANEL = 256
VMEM_LIMIT_BYTES = 62 * 1024 * 1024

F32 = jnp.float32
BF16 = jnp.bfloat16


def _dot(a, b):
    return jnp.dot(a, b, preferred_element_type=F32)


def _dot_nt(a, b):
    return lax.dot_general(a, b, (((1,), (1,)), ((), ())), preferred_element_type=F32)


def _dot_tn(a, b):
    return lax.dot_general(a, b, (((0,), (0,)), ((), ())), preferred_element_type=F32)


def _rms(x):
    return x * lax.rsqrt(jnp.mean(x * x, axis=-1, keepdims=True) + NORM_EPS)


def _sigmoid(x):
    return 0.5 * jnp.tanh(0.5 * x) + 0.5


def _cmul_add(ar, ai, br, bi, cr, ci):
    return ar * br - ai * bi + cr, ar * bi + ai * br + ci


def _interleave(*streams):
    totals = [sum(c for c, _ in s) or 1.0 for s in streams]
    pos = [0] * len(streams)
    done = [0.0] * len(streams)
    while True:
        live = [i for i, s in enumerate(streams) if pos[i] < len(s)]
        if not live:
            return
        i = min(live, key=lambda k: done[k] / totals[k])
        cost, thunk = streams[i][pos[i]]
        thunk()
        pos[i] += 1
        done[i] += cost


def _block_kernel(x_ref, p_ref, w_in_ref, w_o_hg_ref, w_glu_ref, w_o_s5_ref, w_out_ref, w_ple_ref,
                  w_ple_gate_ref, w_b_ref, w_c_ref, perm_ref, perm_t_ref, lam_ref, seg_ref,
                  norm_g_ref, lb_ref, hg_norm_g_ref, b_glu_ref, s5_d_ref, ple_norm_g_ref, final_norm_g_ref,
                  out_ref,
                  u_bf, fb, proj, o_scr, s5p, xs, state, carry, pe_s, usp_s, qa_s, ka_s, qd_s, kd_s, iv_s):
    ts = x_ref.shape[0]
    n_chunks = ts // HG_CHUNK
    subs = [slice(r, r + SUBT) for r in range(0, ts, SUBT)]
    half_in = S5_WIDTH // 2
    half_st = 2 * (S5_STATES // 2)
    n_re = S5_STATES // 2

    @pl.when(pl.program_id(1) == 0)
    def _():
        state[...] = jnp.zeros_like(state)
        carry[...] = jnp.zeros_like(carry)

    def in_proj(src_col, n_cols, sink):
        def thunk(j):
            sink(j, _dot(u_bf[...], w_in_ref[:, src_col + j:src_col + j + MM_PANEL]))
        return [(MM_PANEL * D_MODEL, functools.partial(thunk, j)) for j in range(0, n_cols, MM_PANEL)]

    def store_to(dst):
        def sink(j, val):
            dst[:, j:j + MM_PANEL] = val
        return sink

    u_bf[...] = (_rms(x_ref[...]) * norm_g_ref[...]).astype(BF16)
    for _, thunk in in_proj(COL_F, HG_WIDTH, store_to(fb)):
        thunk()

    sub_row = lax.broadcasted_iota(jnp.int32, (SUBLANES, HG_WIDTH), 0)
    decay = [None] * n_chunks

    def gate_math(c):
        rows = slice(c * HG_CHUNK, (c + 1) * HG_CHUNK)
        lb = lb_ref[...]
        sig = _sigmoid(fb[rows, :])
        k_in = (1.0 - lb) * (1.0 - sig)
        logf = jnp.log(lb + (1.0 - lb) * sig)
        offset = None
        for r in range(0, HG_CHUNK, SUBLANES):
            blk = logf[r:r + SUBLANES, :]
            for d in (1, 2, 4):
                blk = blk + jnp.where(sub_row >= d, pltpu.roll(blk, d, 0), 0.0)
            if offset is not None:
                blk = blk + offset
            offset = jnp.broadcast_to(blk[SUBLANES - 1:SUBLANES, :], (SUBLANES, HG_WIDTH))
            fb[c * HG_CHUNK + r:c * HG_CHUNK + r + SUBLANES, :] = blk
        b = fb[rows, :]
        b_mid = b[HG_CHUNK // 2 - 1:HG_CHUNK // 2, :]
        b_last = b[HG_CHUNK - 1:HG_CHUNK, :]
        ka = k_in * jnp.exp(b_mid - b)
        ka_s[rows, :] = ka.astype(BF16)
        kd_s[rows, :] = (ka * jnp.exp(b_last - b_mid)).astype(BF16)
        decay[c] = jnp.exp(b_last)

    def s5_perm():
        for rs in subs:
            usp_s[rs, :] = _dot(perm_ref[...], s5p[rs, 0:S5_WIDTH].astype(BF16)).astype(BF16)

    lanes = [(hf, slice(hf * half_st + lc, hf * half_st + lc + SCAN_W),
              slice(hf * half_st + n_re + lc, hf * half_st + n_re + lc + SCAN_W), lc)
             for hf in range(2) for lc in range(0, n_re, SCAN_W)]

    def s5_in(n):
        hf, cre, cim, lc = lanes[n]
        u_s = usp_s[:, hf * half_in:(hf + 1) * half_in]
        xs[:, cre] = _dot(u_s, w_b_ref[hf, :, lc:lc + SCAN_W])
        xs[:, cim] = _dot(u_s, w_b_ref[hf, :, n_re + lc:n_re + lc + SCAN_W])

    _interleave([(1.0, functools.partial(gate_math, c)) for c in range(n_chunks)],
                in_proj(COL_S5, 2 * S5_WIDTH, store_to(s5p)) + [(ts * S5_WIDTH, s5_perm),
                                                         (2 * SCAN_W * half_in, functools.partial(s5_in, 0))])

    def q_sink(j, q):
        cols = slice(j, j + MM_PANEL)
        for c in range(n_chunks):
            rows = slice(c * HG_CHUNK, (c + 1) * HG_CHUNK)
            b = fb[rows, cols]
            b_mid = b[HG_CHUNK // 2 - 1:HG_CHUNK // 2, :]
            qa = q[rows, :] * jnp.exp(b - b_mid)
            qa_s[rows, cols] = qa.astype(BF16)
            qd_s[rows, cols] = (qa * jnp.exp(b_mid)).astype(BF16)

    def iv_sink(j, iv):
        iv_s[:, j:j + MM_PANEL] = iv.astype(BF16)

    main_mm = in_proj(COL_TAIL, TAIL_COLS, store_to(proj)) + in_proj(COL_IV, HG_WIDTH, iv_sink)

    sub = lax.broadcasted_iota(jnp.int32, (SUBLANES, SCAN_W), 0)
    y_acc = [None, None]

    def scan_items(n, rs):
        _, cre, cim, _ = lanes[n]
        run = [None]

        def step(i, store):
            rows = slice(rs.start + i * SUBLANES, rs.start + (i + 1) * SUBLANES)
            hr, hi = xs[rows, cre], xs[rows, cim]
            if run[0] is not None:
                hr, hi = _cmul_add(lam_ref[:, cre], lam_ref[:, cim], *run[0], hr, hi)
            if store:
                xs[rows, cre] = hr
                xs[rows, cim] = hi
            run[0] = (hr, hi)

        def seg_starts():
            gr, gi = run[0]
            for k in range(3):
                gr, gi = _cmul_add(seg_ref[k, :, cre], seg_ref[k, :, cim],
                                   pltpu.roll(gr, 1 << k, 0), pltpu.roll(gi, 1 << k, 0), gr, gi)
            gr = jnp.where(sub == 0, 0.0, pltpu.roll(gr, 1, 0))
            gi = jnp.where(sub == 0, 0.0, pltpu.roll(gi, 1, 0))
            run[0] = _cmul_add(seg_ref[3, :, cre], seg_ref[3, :, cim], carry[:, cre], carry[:, cim], gr, gi)

        def carry_out():
            hr, hi = run[0]
            carry[:, cre] = jnp.broadcast_to(hr[SUBLANES - 1:SUBLANES, :], (SUBLANES, SCAN_W))
            carry[:, cim] = jnp.broadcast_to(hi[SUBLANES - 1:SUBLANES, :], (SUBLANES, SCAN_W))

        return ([(1.0, functools.partial(step, i, False)) for i in range(SEG)] + [(1.0, seg_starts)]
                + [(1.0, functools.partial(step, i, True)) for i in range(SEG)] + [(0.1, carry_out)])

    def s5_out_part(n):
        hf, cre, cim, lc = lanes[n]
        part = (_dot(xs[:, cre].astype(BF16), w_c_ref[hf, lc:lc + SCAN_W, :])
                + _dot(xs[:, cim].astype(BF16), w_c_ref[hf, n_re + lc:n_re + lc + SCAN_W, :]))
        y_acc[hf] = part if y_acc[hf] is None else y_acc[hf] + part

    share = -(-len(main_mm) // len(lanes))
    for n in range(len(lanes)):
        beside = []
        if n + 1 < len(lanes):
            beside.append((2 * SCAN_W * half_in, functools.partial(s5_in, n + 1)))
        if n > 0:
            beside.append((2 * SCAN_W * half_in, functools.partial(s5_out_part, n - 1)))
        _interleave([item for rs in subs for item in scan_items(n, rs)],
                    beside + main_mm[n * share:(n + 1) * share])
    s5_out_part(len(lanes) - 1)

    def s5_out_a():
        y_p = jnp.concatenate(y_acc, axis=1)
        y_hi = y_p.astype(BF16)
        y_lo = (y_p - y_hi.astype(F32)).astype(BF16)
        for rs in subs:
            y_s = (_dot(perm_t_ref[...], y_hi[rs, :]) + _dot(perm_t_ref[...], y_lo[rs, :])
                   + s5_d_ref[...] * s5p[rs, 0:S5_WIDTH])
            xs[rs, 0:S5_WIDTH] = jax.nn.gelu(y_s)

    def s5_out_b():
        glu = _dot(xs[:, 0:S5_WIDTH].astype(BF16), w_glu_ref[...]) + b_glu_ref[...]
        z_s = s5p[:, S5_WIDTH:2 * S5_WIDTH]
        ys = glu[:, 0:S5_WIDTH] * _sigmoid(glu[:, S5_WIDTH:2 * S5_WIDTH]) * (z_s * _sigmoid(z_s))
        xs[:, D_MODEL:2 * D_MODEL] = _dot(ys.astype(BF16), w_o_s5_ref[...])

    def ple():
        pe_s[...] = _dot(p_ref[...].astype(BF16), w_ple_ref[...])

    _interleave(in_proj(COL_Q, HG_WIDTH, q_sink), [(2.0, s5_out_a), (2.0, s5_out_b), (0.5, ple)])

    row = lax.broadcasted_iota(jnp.int32, (SUBT, SUBT), 0)
    col = lax.broadcasted_iota(jnp.int32, (SUBT, SUBT), 1)
    causal = (col <= row) & (col >= (row & -HG_CHUNK))

    for h in range(HG_HEADS):
        for rs in subs:
            sl = slice(h * HG_KEY, (h + 1) * HG_KEY)
            scores = jnp.where(causal, _dot_nt(qa_s[rs, sl], ka_s[rs, sl]), 0.0)
            o_scr[rs, sl] = _dot(scores.astype(BF16), iv_s[rs, sl])
    for c in range(n_chunks):
        rows = slice(c * HG_CHUNK, (c + 1) * HG_CHUNK)
        for h in range(HG_HEADS):
            sl = slice(h * HG_KEY, (h + 1) * HG_KEY)
            st = state[h]
            o = o_scr[rows, sl] + _dot_nt(qd_s[rows, sl], st.astype(BF16))
            state[h] = st * decay[c][:, sl] + _dot_tn(iv_s[rows, sl], kd_s[rows, sl])
            o_scr[rows, sl] = _rms(o)

    g_hg = proj[:, P_G:P_G + HG_WIDTH]
    o_act = o_scr[...] * hg_norm_g_ref[...] * (g_hg * _sigmoid(g_hg))
    y_hg = _dot(o_act.astype(BF16), w_o_hg_ref[...])
    merged = (_sigmoid(proj[:, P_GATE_HG:P_GATE_HG + D_MODEL]) * y_hg
              + _sigmoid(proj[:, P_GATE_S5:P_GATE_S5 + D_MODEL]) * xs[:, D_MODEL:2 * D_MODEL])
    h = x_ref[...] + _dot(merged.astype(BF16), w_out_ref[...])
    gate = _sigmoid(_dot((_rms(h) * ple_norm_g_ref[...]).astype(BF16), w_ple_gate_ref[...]))
    h = h + pe_s[...] * gate
    out_ref[...] = _rms(h) * final_norm_g_ref[...]


def _s5_tables(a_re, a_im, log_dt, b_re, b_im, c_re, c_im):
    g, n, c = S5_GROUPS, S5_STATE, S5_GROUP
    gh, hs, hi_ = g // 2, S5_STATES // 2, S5_WIDTH // 2
    hp = lax.Precision.HIGHEST
    dt = jnp.exp(log_dt.astype(F32))[:, None]
    ar, ai = a_re.astype(F32), a_im.astype(F32)

    m = jnp.asarray([1.0] + [float(SEG * j) for j in range(SUBLANES)], F32)[:, None, None]
    mag, ang = jnp.exp(m * (ar * dt)), m * (ai * dt)
    pr, pi = mag * jnp.cos(ang), mag * jnp.sin(ang)
    lr, li = pr[0], pi[0]
    den = ar * ar + ai * ai
    nr = lr - 1.0
    sr = (nr * ar + li * ai) / den
    si = (li * ar - nr * ai) / den
    br, bi = b_re.astype(F32), b_im.astype(F32)
    bbr = sr[..., None] * br - si[..., None] * bi
    bbi = sr[..., None] * bi + si[..., None] * br

    rows_c, cols_n = jnp.arange(hi_), jnp.arange(hs)
    a_in = jnp.stack([bbr, bbi]).reshape(2, 2, gh, n, c).transpose(1, 0, 2, 4, 3).reshape(2, 2, hi_, n)
    tile_n = (cols_n[None, :] % n == jnp.arange(n)[:, None]).astype(F32)
    t_in = jnp.einsum('hprn,nm->hprm', a_in, tile_n, precision=hp)
    t_in = jnp.where(rows_c[:, None] // c == cols_n[None, :] // n, t_in, 0.0)
    w_b = jnp.concatenate([t_in[:, 0], t_in[:, 1]], axis=-1).astype(BF16)

    a_out = jnp.stack([c_re.astype(F32), -c_im.astype(F32)]).reshape(2, 2, gh, c, n)
    a_out = a_out.transpose(1, 0, 2, 4, 3).reshape(2, 2, hs, c)
    tile_c = (rows_c[None, :] % c == jnp.arange(c)[:, None]).astype(F32)
    t_out = jnp.einsum('hprc,cm->hprm', a_out, tile_c, precision=hp)
    t_out = jnp.where(cols_n[:, None] // n == rows_c[None, :] // c, t_out, 0.0)
    w_c = t_out.reshape(2, 2 * hs, hi_).astype(BF16)

    pr, pi = pr.reshape(-1, S5_STATES), pi.reshape(-1, S5_STATES)
    pw = jnp.concatenate([pr[:, :hs], pi[:, :hs], pr[:, hs:], pi[:, hs:]], axis=1)
    rows = jnp.arange(SUBLANES)[:, None]
    bcast = lambda row: jnp.broadcast_to(row[None, :], (SUBLANES, 2 * S5_STATES))
    lam = bcast(pw[0])
    sp = pw[1:]
    seg = jnp.stack([jnp.where(rows >= k, bcast(sp[k]), 0.0) for k in (1, 2, 4)] + [sp])
    return w_b, w_c, lam, seg


def kernel(x, p, norm_g, w_in, hg_lb, hg_norm_g, w_o_hg, s5_a_re, s5_a_im, s5_log_dt, s5_b_re, s5_b_im,
           s5_c_re, s5_c_im, s5_d, w_glu, b_glu, w_o_s5, w_out, ple_norm_g, w_ple, w_ple_gate, final_norm_g):
    bsz, seq, d = x.shape
    depth = w_in.shape[0]
    assert depth == 1 and d == D_MODEL and seq % TS == 0 and TS % SUBT == 0 and SUBT % HG_CHUNK == 0
    l = 0
    lb = jnp.cumsum(jax.nn.softmax(hg_lb.astype(F32), axis=0), axis=0)[l].reshape(1, HG_WIDTH)
    w_b, w_c, lam, seg = _s5_tables(s5_a_re[l], s5_a_im[l], s5_log_dt[l], s5_b_re[l], s5_b_im[l],
                                        s5_c_re[l], s5_c_im[l])
    r = jnp.arange(SUBT)
    perm = (r[None, :] == (r[:, None] % SUBLANES) * SEG + r[:, None] // SUBLANES).astype(BF16)

    w_in_l = w_in[l].astype(BF16)
    q_c, f_c, i_c, g_c, s5_c, gate_c = (slice(a, b) for a, b in (
        (0, 1024), (1024, 2048), (2048, 3072), (3072, 4096), (4096, 5120), (5120, 7168)))
    w_in_k = jnp.concatenate([w_in_l[:, c] for c in (f_c, s5_c, q_c, i_c, g_c, gate_c)], axis=1)
    weights = [w_in_k, w_o_hg[l].astype(BF16), w_glu[l].astype(BF16), w_o_s5[l].astype(BF16),
               w_out[l].astype(BF16), w_ple[l].astype(BF16), w_ple_gate[l].astype(BF16), w_b, w_c,
               perm, perm.T, lam, seg]
    rows_ = [norm_g[l].reshape(1, d), lb, hg_norm_g[l].reshape(1, HG_WIDTH), b_glu[l].reshape(1, 2 * S5_WIDTH),
             s5_d[l].reshape(1, S5_WIDTH), ple_norm_g[l].reshape(1, d), final_norm_g.reshape(1, d)]
    rows_ = [a.astype(F32) for a in rows_]

    def resident(a):
        nd = a.ndim
        return pl.BlockSpec(a.shape, lambda b, s, _nd=nd: (0,) * _nd, pipeline_mode=pl.Buffered(1))

    in_specs = ([pl.BlockSpec((None, TS, d), lambda b, s: (b, s, 0)),
                 pl.BlockSpec((None, TS, PLE_DIM), lambda b, s: (b, s, 0))]
                + [resident(a) for a in weights] + [resident(a) for a in rows_])
    scratch = [
        pltpu.VMEM((TS, d), BF16),
        pltpu.VMEM((TS, HG_WIDTH), F32),
        pltpu.VMEM((TS, TAIL_COLS), F32),
        pltpu.VMEM((TS, HG_WIDTH), F32),
        pltpu.VMEM((TS, 2 * S5_WIDTH), F32),
        pltpu.VMEM((TS, 2 * S5_STATES), F32),
        pltpu.VMEM((HG_HEADS, HG_VAL, HG_KEY), F32),
        pltpu.VMEM((SUBLANES, 2 * S5_STATES), F32),
        pltpu.VMEM((TS, D_MODEL), F32),
        pltpu.VMEM((TS, S5_WIDTH), BF16),
    ] + [pltpu.VMEM((TS, HG_WIDTH), BF16)] * 5
    return pl.pallas_call(
        _block_kernel,
        out_shape=jax.ShapeDtypeStruct((bsz, seq, d), x.dtype),
        grid=(bsz, seq // TS),
        in_specs=in_specs,
        out_specs=pl.BlockSpec((None, TS, d), lambda b, s: (b, s, 0)),
        scratch_shapes=scratch,
        compiler_params=pltpu.CompilerParams(dimension_semantics=("arbitrary", "arbitrary"),
                                             vmem_limit_bytes=VMEM_LIMIT_BYTES),
        name="hgrn2_s5_block",
    )(x, p[l], *weights, *rows_)
```
